```python
import math
import jax, jax.numpy as jnp
from jax import lax
import numpy as np

D_MODEL = 2048
BATCH = 4
SEQ = 2048
DEPTH = 2
DEC_BATCH = 2
DEC_SEQ = 4096
PAST_LEN = 128

N_MIXERS = 2
N_CONV_LAYERS = (DEPTH + N_MIXERS - 1) // N_MIXERS
N_SSM_LAYERS = DEPTH // N_MIXERS
CONV_WIDTH = 31
CONV_PAD = (CONV_WIDTH - 1) // 2
SSM_GROUP = 16
SSM_GROUPS = D_MODEL // SSM_GROUP
SSM_STATE = 64
N_DIRS = 2
N_EXPERTS = 16
EC_CAPACITY_FACTOR = 2
D_FF_EXPERT = D_MODEL
RMS_EPS = 1e-6
LN_EPS = 1e-5
DT_MIN = 1e-3
DT_MAX = 1e-1

kernel_name = 'hybrid_conv_s5_ecmoe_encoder'


def rms_norm(x, g):
    xf = x.astype(jnp.float32)
    y = xf * lax.rsqrt(jnp.mean(xf * xf, axis=-1, keepdims=True) + RMS_EPS)
    return (y * g.astype(jnp.float32)).astype(x.dtype)


def layer_norm(x, g, b):
    xf = x.astype(jnp.float32)
    mu = jnp.mean(xf, axis=-1, keepdims=True)
    xc = xf - mu
    y = xc * lax.rsqrt(jnp.mean(xc * xc, axis=-1, keepdims=True) + LN_EPS)
    return (y * g.astype(jnp.float32) + b.astype(jnp.float32)).astype(x.dtype)


def modulate(h, shift, scale):
    return h * (1 + scale[:, None, :]) + shift[:, None, :]


def conv_module(h, w_in, b_in, w_dw, b_dw, ln_g, ln_b, w_out, b_out):
    u = h @ w_in + b_in
    a, g = jnp.split(u, 2, axis=-1)
    u = a * jax.nn.sigmoid(g)
    d = u.shape[-1]
    u = lax.conv_general_dilated(
        u, w_dw[:, None, :].astype(u.dtype), window_strides=(1,),
        padding=[(CONV_PAD, CONV_PAD)],
        dimension_numbers=('NWC', 'WIO', 'NWC'),
        feature_group_count=d) + b_dw
    u = jax.nn.silu(layer_norm(u, ln_g, ln_b))
    return u @ w_out + b_out


def zoh_discretise(lam_re, lam_im, log_step, b_re, b_im):
    lam_re = lam_re.astype(jnp.float32)
    lam_im = lam_im.astype(jnp.float32)
    b_re = b_re.astype(jnp.float32)
    b_im = b_im.astype(jnp.float32)
    dt = jnp.exp(log_step.astype(jnp.float32))[:, None]
    mag = jnp.exp(lam_re * dt)
    a_re = mag * jnp.cos(lam_im * dt)
    a_im = mag * jnp.sin(lam_im * dt)
    nr = a_re - 1.0
    ni = a_im
    den = lam_re * lam_re + lam_im * lam_im
    k_re = (nr * lam_re + ni * lam_im) / den
    k_im = (ni * lam_re - nr * lam_im) / den
    bb_re = k_re[..., None] * b_re - k_im[..., None] * b_im
    bb_im = k_re[..., None] * b_im + k_im[..., None] * b_re
    return a_re, a_im, bb_re, bb_im


def _ssm_combine(left, right):
    a1r, a1i, b1r, b1i = left
    a2r, a2i, b2r, b2i = right
    ar = a1r * a2r - a1i * a2i
    ai = a1r * a2i + a1i * a2r
    br = a2r * b1r - a2i * b1i + b2r
    bi = a2r * b1i + a2i * b1r + b2i
    return ar, ai, br, bi


def s5_direction(u, lam_re, lam_im, log_step, b_re, b_im, c_re, c_im, reverse):
    a_re, a_im, bb_re, bb_im = zoh_discretise(lam_re, lam_im, log_step, b_re, b_im)
    bu_re = jnp.einsum('blgc,gpc->blgp', u, bb_re)
    bu_im = jnp.einsum('blgc,gpc->blgp', u, bb_im)
    shp = (1, u.shape[1]) + a_re.shape
    _, _, x_re, x_im = lax.associative_scan(
        _ssm_combine,
        (jnp.broadcast_to(a_re, shp), jnp.broadcast_to(a_im, shp), bu_re, bu_im),
        reverse=reverse, axis=1)
    c_re = c_re.astype(jnp.float32)
    c_im = c_im.astype(jnp.float32)
    return (jnp.einsum('blgp,gcp->blgc', x_re, c_re)
            - jnp.einsum('blgp,gcp->blgc', x_im, c_im))


def s5_module(h, lam_re, lam_im, log_step, b_re, b_im, c_re, c_im, d_skip, w_glu, b_glu):
    bsz, seq, dm = h.shape
    hf = h.astype(jnp.float32)
    u = hf.reshape(bsz, seq, SSM_GROUPS, SSM_GROUP)
    y = (s5_direction(u, lam_re[0], lam_im[0], log_step[0], b_re[0], b_im[0], c_re[0], c_im[0], False)
         + s5_direction(u, lam_re[1], lam_im[1], log_step[1], b_re[1], b_im[1], c_re[1], c_im[1], True))
    y = y.reshape(bsz, seq, dm) + d_skip.astype(jnp.float32) * hf
    y = jax.nn.gelu(y).astype(h.dtype)
    v = y @ w_glu + b_glu
    a, g = jnp.split(v, 2, axis=-1)
    return a * jax.nn.sigmoid(g)


def ec_moe(h, w_router, w_gate, w_up, w_down):
    bsz, seq, dm = h.shape
    n_tok = bsz * seq
    cap = EC_CAPACITY_FACTOR * n_tok // N_EXPERTS
    t = h.reshape(n_tok, dm)
    aff = jax.nn.softmax((t @ w_router).astype(jnp.float32), axis=-1)
    gate, idx = lax.top_k(aff.T, cap)
    xs = t[idx]
    hid = (jax.nn.silu(jnp.einsum('ecd,edf->ecf', xs, w_gate))
           * jnp.einsum('ecd,edf->ecf', xs, w_up))
    out = jnp.einsum('ecf,efd->ecd', hid, w_down) * gate[..., None].astype(h.dtype)
    y = jnp.zeros((n_tok, dm), h.dtype).at[idx.reshape(-1)].add(out.reshape(-1, dm))
    return y.reshape(bsz, seq, dm)


def trunk(x, c, p):
    cond = jax.nn.silu(c)
    for i in range(DEPTH):
        mod = cond @ p['ada_w'][i] + p['ada_b'][i]
        sh1, sc1, g1, sh2, sc2, g2 = jnp.split(mod, 6, axis=-1)
        h = modulate(rms_norm(x, p['norm_mix_g'][i]), sh1, sc1)
        j = i // N_MIXERS
        if i % N_MIXERS == 0:
            o = conv_module(h, p['conv_w_in'][j], p['conv_b_in'][j], p['conv_w_dw'][j], p['conv_b_dw'][j],
                            p['conv_ln_g'][j], p['conv_ln_b'][j], p['conv_w_out'][j], p['conv_b_out'][j])
        else:
            o = s5_module(h, p['ssm_lambda_re'][j], p['ssm_lambda_im'][j], p['ssm_log_step'][j],
                          p['ssm_b_re'][j], p['ssm_b_im'][j], p['ssm_c_re'][j], p['ssm_c_im'][j],
                          p['ssm_d'][j], p['ssm_w_glu'][j], p['ssm_b_glu'][j])
        x = x + g1[:, None, :] * o
        h = modulate(rms_norm(x, p['norm_ffn_g'][i]), sh2, sc2)
        x = x + g2[:, None, :] * ec_moe(h, p['moe_w_router'][i], p['moe_w_gate'][i],
                                        p['moe_w_up'][i], p['moe_w_down'][i])
    return rms_norm(x, p['final_norm_g'])


def setup_inputs(seed: int = 0) -> dict:
    key = jax.random.key(seed)
    ks = jax.random.split(key, 31)
    f32 = jnp.float32
    D = D_MODEL

    def nrm(k, shape, scale):
        return jax.random.normal(k, shape, f32) * scale

    n_idx = jnp.arange(SSM_STATE, dtype=f32)
    ssm_shape = (N_SSM_LAYERS, N_DIRS, SSM_GROUPS, SSM_STATE)
    return {
        'x_prompt': nrm(ks[0], (BATCH, SEQ, D), 1.0),
        'x_sample': nrm(ks[1], (DEC_BATCH, DEC_SEQ, D), 1.0),
        'c_prompt': nrm(ks[2], (BATCH, D), 1.0),
        'c_sample': nrm(ks[3], (DEC_BATCH, D), 1.0),
        'ada_w': nrm(ks[4], (DEPTH, D, 6 * D), 0.5 * D ** -0.5),
        'ada_b': nrm(ks[5], (DEPTH, 6 * D), 0.02),
        'norm_mix_g': 1.0 + nrm(ks[6], (DEPTH, D), 0.02),
        'norm_ffn_g': 1.0 + nrm(ks[7], (DEPTH, D), 0.02),
        'final_norm_g': 1.0 + nrm(ks[8], (D,), 0.02),
        'conv_w_in': nrm(ks[9], (N_CONV_LAYERS, D, 2 * D), D ** -0.5),
        'conv_b_in': nrm(ks[10], (N_CONV_LAYERS, 2 * D), 0.02),
        'conv_w_dw': nrm(ks[11], (N_CONV_LAYERS, CONV_WIDTH, D), CONV_WIDTH ** -0.5),
        'conv_b_dw': nrm(ks[12], (N_CONV_LAYERS, D), 0.02),
        'conv_ln_g': 1.0 + nrm(ks[13], (N_CONV_LAYERS, D), 0.02),
        'conv_ln_b': nrm(ks[14], (N_CONV_LAYERS, D), 0.02),
        'conv_w_out': nrm(ks[15], (N_CONV_LAYERS, D, D), D ** -0.5),
        'conv_b_out': nrm(ks[16], (N_CONV_LAYERS, D), 0.02),
        'ssm_lambda_re': -0.5 + nrm(ks[17], ssm_shape, 0.01),
        'ssm_lambda_im': math.pi * n_idx + nrm(ks[18], ssm_shape, 0.01),
        'ssm_log_step': jax.random.uniform(ks[19], (N_SSM_LAYERS, N_DIRS, SSM_GROUPS), f32,
                                           math.log(DT_MIN), math.log(DT_MAX)),
        'ssm_b_re': nrm(ks[20], ssm_shape + (SSM_GROUP,), (2 * SSM_GROUP) ** -0.5),
        'ssm_b_im': nrm(ks[21], ssm_shape + (SSM_GROUP,), (2 * SSM_GROUP) ** -0.5),
        'ssm_c_re': nrm(ks[22], (N_SSM_LAYERS, N_DIRS, SSM_GROUPS, SSM_GROUP, SSM_STATE), (2 * SSM_STATE) ** -0.5),
        'ssm_c_im': nrm(ks[23], (N_SSM_LAYERS, N_DIRS, SSM_GROUPS, SSM_GROUP, SSM_STATE), (2 * SSM_STATE) ** -0.5),
        'ssm_d': nrm(ks[24], (N_SSM_LAYERS, D), 1.0),
        'ssm_w_glu': nrm(ks[25], (N_SSM_LAYERS, D, 2 * D), D ** -0.5),
        'ssm_b_glu': nrm(ks[26], (N_SSM_LAYERS, 2 * D), 0.02),
        'moe_w_router': nrm(ks[27], (DEPTH, D, N_EXPERTS), D ** -0.5),
        'moe_w_gate': nrm(ks[28], (DEPTH, N_EXPERTS, D, D_FF_EXPERT), D ** -0.5),
        'moe_w_up': nrm(ks[29], (DEPTH, N_EXPERTS, D, D_FF_EXPERT), D ** -0.5),
        'moe_w_down': nrm(ks[30], (DEPTH, N_EXPERTS, D_FF_EXPERT, D), D_FF_EXPERT ** -0.5),
    }


def reference(x_prompt, x_sample, c_prompt, c_sample, ada_w, ada_b, norm_mix_g, norm_ffn_g, final_norm_g,
              conv_w_in, conv_b_in, conv_w_dw, conv_b_dw, conv_ln_g, conv_ln_b, conv_w_out, conv_b_out,
              ssm_lambda_re, ssm_lambda_im, ssm_log_step, ssm_b_re, ssm_b_im, ssm_c_re, ssm_c_im,
              ssm_d, ssm_w_glu, ssm_b_glu, moe_w_router, moe_w_gate, moe_w_up, moe_w_down):
    p = {
        'ada_w': ada_w, 'ada_b': ada_b, 'norm_mix_g': norm_mix_g, 'norm_ffn_g': norm_ffn_g,
        'final_norm_g': final_norm_g,
        'conv_w_in': conv_w_in, 'conv_b_in': conv_b_in, 'conv_w_dw': conv_w_dw, 'conv_b_dw': conv_b_dw,
        'conv_ln_g': conv_ln_g, 'conv_ln_b': conv_ln_b, 'conv_w_out': conv_w_out, 'conv_b_out': conv_b_out,
        'ssm_lambda_re': ssm_lambda_re, 'ssm_lambda_im': ssm_lambda_im, 'ssm_log_step': ssm_log_step,
        'ssm_b_re': ssm_b_re, 'ssm_b_im': ssm_b_im, 'ssm_c_re': ssm_c_re, 'ssm_c_im': ssm_c_im,
        'ssm_d': ssm_d, 'ssm_w_glu': ssm_w_glu, 'ssm_b_glu': ssm_b_glu,
        'moe_w_router': moe_w_router, 'moe_w_gate': moe_w_gate, 'moe_w_up': moe_w_up, 'moe_w_down': moe_w_down,
    }
    y_prompt = trunk(x_prompt, c_prompt, p)
    y_sample = trunk(x_sample, c_sample, p)
    return (y_prompt, y_sample)
```

```python
import functools
import math

import jax
import jax.numpy as jnp
from jax import lax
from jax.experimental import pallas as pl
from jax.experimental.pallas import tpu as pltpu

F32 = jnp.float32
BF16 = jnp.bfloat16
I32 = jnp.int32
HIGHEST = lax.Precision.HIGHEST

RMS_EPS = 1e-6
LN_EPS = 1e-5
CONV_WIDTH = 31
CONV_PAD = (CONV_WIDTH - 1) // 2
CONV_HALO = 16
SSM_GROUP = 16
SSM_STATE = 64
SSM_BLOCKS = 8
N_EXPERTS = 16
EC_CAPACITY_FACTOR = 2
LANES = 128
SLOT_BLOCK = 128

VMEM_LIMIT = 56 * 1024 * 1024


def _cparams(sem, vmem=VMEM_LIMIT):
    return pltpu.CompilerParams(dimension_semantics=sem, vmem_limit_bytes=vmem)


def _modnorm(x, g, sh, sc):
    ms = jnp.mean(x * x, axis=-1, keepdims=True)
    y = x * lax.rsqrt(ms + RMS_EPS) * g
    return y * (1.0 + sc) + sh


def _silu(x):
    return x * jax.nn.sigmoid(x)


def _gelu_tanh(x):
    c = math.sqrt(2.0 / math.pi)
    return 0.5 * x * (1.0 + jnp.tanh(c * (x + 0.044715 * (x * x * x))))


def _ada_kernel(c_ref, w_ref, b_ref, o_ref):
    c = c_ref[...]
    cond = _silu(c)
    o_ref[0] = jnp.dot(cond, w_ref[0], precision=HIGHEST, preferred_element_type=F32) + b_ref[0]


def _ada(c_all, ada_w, ada_b):
    depth, d, n6 = ada_w.shape
    rows = c_all.shape[0]
    tn = _pick(n6, 1024)
    return pl.pallas_call(
        _ada_kernel,
        grid=(depth, n6 // tn),
        in_specs=[
            pl.BlockSpec((rows, d), lambda i, j: (0, 0)),
            pl.BlockSpec((1, d, tn), lambda i, j: (i, 0, j)),
            pl.BlockSpec((1, 1, tn), lambda i, j: (i, 0, j)),
        ],
        out_specs=pl.BlockSpec((1, rows, tn), lambda i, j: (i, 0, j)),
        out_shape=jax.ShapeDtypeStruct((depth, rows, n6), F32),
        compiler_params=_cparams(("parallel", "parallel")),
        name="ada_mod",
    )(c_all, ada_w, ada_b.reshape(depth, 1, n6))


def _conv_in_kernel(x_ref, g_ref, sh_ref, sc_ref, wa_ref, wg_ref, ba_ref, bg_ref, o_ref, h_scr):
    @pl.when(pl.program_id(2) == 0)
    def _():
        h_scr[...] = _modnorm(x_ref[0], g_ref[...], sh_ref[0], sc_ref[0]).astype(BF16)

    h = h_scr[...]
    a = jnp.dot(h, wa_ref[...], preferred_element_type=F32) + ba_ref[...]
    g = jnp.dot(h, wg_ref[...], preferred_element_type=F32) + bg_ref[...]
    o_ref[0] = a * jax.nn.sigmoid(g)


def _conv_in(x, norm_g, mod, w_in_bf, b_in, tm, tn):
    b, l, d = x.shape
    nj = d // tn
    return pl.pallas_call(
        _conv_in_kernel,
        grid=(b, l // tm, nj),
        in_specs=[
            pl.BlockSpec((1, tm, d), lambda bi, i, j: (bi, i, 0)),
            pl.BlockSpec((1, d), lambda bi, i, j: (0, 0)),
            pl.BlockSpec((1, 1, d), lambda bi, i, j: (bi, 0, 0)),
            pl.BlockSpec((1, 1, d), lambda bi, i, j: (bi, 0, 1)),
            pl.BlockSpec((d, tn), lambda bi, i, j: (0, j)),
            pl.BlockSpec((d, tn), lambda bi, i, j: (0, j + nj)),
            pl.BlockSpec((1, tn), lambda bi, i, j: (0, j)),
            pl.BlockSpec((1, tn), lambda bi, i, j: (0, j + nj)),
        ],
        out_specs=pl.BlockSpec((1, tm, tn), lambda bi, i, j: (bi, i, j)),
        out_shape=jax.ShapeDtypeStruct((b, l, d), F32),
        scratch_shapes=[pltpu.VMEM((tm, d), BF16)],
        compiler_params=_cparams(("parallel", "parallel", "arbitrary")),
        name="conv_in_glu",
    )(x, norm_g.reshape(1, d), mod, mod, w_in_bf, w_in_bf, b_in.reshape(1, 2 * d), b_in.reshape(1, 2 * d))


def _conv_out_kernel(u_ref, up_ref, un_ref, wdw_ref, bdw_ref, lg_ref, lb_ref, wo_ref, bo_ref, x_ref, g1_ref,
                     o_ref, ext_scr, c_scr, v_scr, *, tm, d, rc, cc):
    i = pl.program_id(1)
    ni = pl.num_programs(1)

    ncol = d // cc

    @pl.when(pl.program_id(2) == 0)
    def _():
        for c in range(ncol):
            cols = pl.ds(c * cc, cc)
            ext_scr[c, pl.ds(0, CONV_HALO), :] = jnp.where(i > 0, up_ref[0, :, cols], 0.0)
            ext_scr[c, pl.ds(CONV_HALO, tm), :] = u_ref[0, :, cols]
            ext_scr[c, pl.ds(CONV_HALO + tm, CONV_HALO), :] = jnp.where(i < ni - 1, un_ref[0, :, cols], 0.0)

        def col_chunk(c, _):
            for r0 in range(0, tm, rc):
                acc = jnp.zeros((rc, cc), F32) + bdw_ref[c]
                for k in range(CONV_WIDTH):
                    seg = ext_scr[c, pl.ds(r0 + (CONV_HALO - CONV_PAD + k), rc), :]
                    acc = acc + seg * wdw_ref[c, pl.ds(k, 1), :]
                c_scr[c, pl.ds(r0, rc), :] = acc
            return 0

        lax.fori_loop(0, ncol, col_chunk, 0)

        def ln_chunk(r, _):
            r0 = pl.multiple_of(r * rc, rc)
            xc = [c_scr[c, pl.ds(r0, rc), :] for c in range(ncol)]
            mu = sum(jnp.sum(v, axis=-1, keepdims=True) for v in xc) * (1.0 / d)
            xm = [v - mu for v in xc]
            var = sum(jnp.sum(v * v, axis=-1, keepdims=True) for v in xm) * (1.0 / d)
            inv = lax.rsqrt(var + LN_EPS)
            for c in range(ncol):
                cols = pl.ds(c * cc, cc)
                y = xm[c] * inv * lg_ref[:, cols] + lb_ref[:, cols]
                v_scr[pl.ds(r0, rc), cols] = _silu(y).astype(BF16)
            return 0

        lax.fori_loop(0, tm // rc, ln_chunk, 0)

    o = jnp.dot(v_scr[...], wo_ref[...], preferred_element_type=F32) + bo_ref[...]
    o_ref[0] = x_ref[0] + g1_ref[0] * o


def _conv_out(u, x, mod, w_dw, b_dw, ln_g, ln_b, w_out_bf, b_out, tm, tn):
    b, l, d = x.shape
    nj = d // tn
    hb = tm // CONV_HALO
    nhb = l // CONV_HALO
    rc = min(64, tm)
    cc = min(256, d)
    ncol = d // cc
    w_dw_p = jnp.concatenate([w_dw, jnp.zeros((1, d), w_dw.dtype)], axis=0)
    w_dw_p = w_dw_p.reshape(CONV_WIDTH + 1, ncol, cc).transpose(1, 0, 2)
    kern = functools.partial(_conv_out_kernel, tm=tm, d=d, rc=rc, cc=cc)
    return pl.pallas_call(
        kern,
        grid=(b, l // tm, nj),
        in_specs=[
            pl.BlockSpec((1, tm, d), lambda bi, i, j: (bi, i, 0)),
            pl.BlockSpec((1, CONV_HALO, d), lambda bi, i, j: (bi, jnp.maximum(i * hb - 1, 0), 0)),
            pl.BlockSpec((1, CONV_HALO, d), lambda bi, i, j: (bi, jnp.minimum((i + 1) * hb, nhb - 1), 0)),
            pl.BlockSpec((ncol, CONV_WIDTH + 1, cc), lambda bi, i, j: (0, 0, 0)),
            pl.BlockSpec((ncol, 1, cc), lambda bi, i, j: (0, 0, 0)),
            pl.BlockSpec((1, d), lambda bi, i, j: (0, 0)),
            pl.BlockSpec((1, d), lambda bi, i, j: (0, 0)),
            pl.BlockSpec((d, tn), lambda bi, i, j: (0, j)),
            pl.BlockSpec((1, tn), lambda bi, i, j: (0, j)),
            pl.BlockSpec((1, tm, tn), lambda bi, i, j: (bi, i, j)),
            pl.BlockSpec((1, 1, tn), lambda bi, i, j: (bi, 0, 2 * nj + j)),
        ],
        out_specs=pl.BlockSpec((1, tm, tn), lambda bi, i, j: (bi, i, j)),
        out_shape=jax.ShapeDtypeStruct((b, l, d), F32),
        scratch_shapes=[
            pltpu.VMEM((ncol, tm + 2 * CONV_HALO, cc), F32),
            pltpu.VMEM((ncol, tm, cc), F32),
            pltpu.VMEM((tm, d), BF16),
        ],
        compiler_params=_cparams(("parallel", "parallel", "arbitrary")),
        name="conv_dw_ln_out",
    )(u, u, u, w_dw_p, b_dw.reshape(ncol, 1, cc), ln_g.reshape(1, d), ln_b.reshape(1, d), w_out_bf,
      b_out.reshape(1, d), x, mod)


def _zoh_discretise(lam_re, lam_im, log_step, b_re, b_im):
    dt = jnp.exp(log_step)[..., None]
    mag = jnp.exp(lam_re * dt)
    a_re = mag * jnp.cos(lam_im * dt)
    a_im = mag * jnp.sin(lam_im * dt)
    nr = a_re - 1.0
    ni = a_im
    den = lam_re * lam_re + lam_im * lam_im
    k_re = (nr * lam_re + ni * lam_im) / den
    k_im = (ni * lam_re - nr * lam_im) / den
    bb_re = k_re[..., None] * b_re - k_im[..., None] * b_im
    bb_im = k_re[..., None] * b_im + k_im[..., None] * b_re
    return a_re, a_im, bb_re, bb_im


def _s5_layout(lam_re, lam_im, log_step, b_re, b_im, c_re, c_im):
    a_re, a_im, bb_re, bb_im = _zoh_discretise(lam_re, lam_im, log_step, b_re, b_im)
    nd, g, p = a_re.shape
    c = b_re.shape[-1]
    gpb = g // SSM_BLOCKS
    eye = jnp.eye(gpb, dtype=F32)

    def expand(bb):
        bb = bb.reshape(nd, SSM_BLOCKS, gpb, p, c)
        return jnp.einsum('dsgpc,gh->dsgchp', bb, eye).reshape(nd, SSM_BLOCKS, gpb * c, gpb * p).astype(BF16)

    def contract(cc):
        cc = cc.reshape(nd, SSM_BLOCKS, gpb, c, p)
        return jnp.einsum('dsgcp,gh->dshpgc', cc, eye).reshape(nd, SSM_BLOCKS, gpb * p, gpb * c).astype(BF16)

    return (a_re.reshape(nd, SSM_BLOCKS, gpb * p), a_im.reshape(nd, SSM_BLOCKS, gpb * p),
            expand(bb_re), expand(bb_im), contract(c_re), contract(c_im))


def _s5_kernel(x_ref, g_ref, sh_ref, sc_ref, ar_ref, ai_ref, wer_ref, wei_ref, wcr_ref, wci_ref,
               o_ref, sre, sim, cre, cim, *, t, chb, nsb, pitch):
    dirn = pl.program_id(0)
    nslab = nsb // LANES

    @pl.when(pl.program_id(2) == 0)
    def _():
        cre[...] = jnp.zeros_like(cre)
        cim[...] = jnp.zeros_like(cim)

    h = _modnorm(x_ref[0], g_ref[...], sh_ref[0], sc_ref[0]).astype(BF16)

    for s in range(SSM_BLOCKS):
        hs = h[:, s * chb:(s + 1) * chb]
        bur = jnp.dot(hs, wer_ref[0, s], preferred_element_type=F32)
        bui = jnp.dot(hs, wei_ref[0, s], preferred_element_type=F32)
        for j in range(nslab):
            sre[j, pl.ds(s * pitch, t), :] = bur[:, j * LANES:(j + 1) * LANES]
            sim[j, pl.ds(s * pitch, t), :] = bui[:, j * LANES:(j + 1) * LANES]

    ar = [ar_ref[0, :, j * LANES:(j + 1) * LANES] for j in range(nslab)]
    ai = [ai_ref[0, :, j * LANES:(j + 1) * LANES] for j in range(nslab)]

    def step(k, carry):
        xr, xi = carry
        tt = jnp.where(dirn == 0, k, t - 1 - k)
        nr, ni = [], []
        for j in range(nslab):
            rows = pl.ds(tt, SSM_BLOCKS, stride=pitch)
            br = sre.at[j][rows, :]
            bi = sim.at[j][rows, :]
            r = ar[j] * xr[j] - ai[j] * xi[j] + br
            im = ar[j] * xi[j] + ai[j] * xr[j] + bi
            sre.at[j][rows, :] = r
            sim.at[j][rows, :] = im
            nr.append(r)
            ni.append(im)
        return tuple(nr), tuple(ni)

    x0 = (tuple(cre[:, j * LANES:(j + 1) * LANES] for j in range(nslab)),
          tuple(cim[:, j * LANES:(j + 1) * LANES] for j in range(nslab)))
    xr, xi = lax.fori_loop(0, t, step, x0)
    for j in range(nslab):
        cre[:, j * LANES:(j + 1) * LANES] = xr[j]
        cim[:, j * LANES:(j + 1) * LANES] = xi[j]

    for s in range(SSM_BLOCKS):
        sr = jnp.concatenate([sre[j, pl.ds(s * pitch, t), :] for j in range(nslab)], axis=1).astype(BF16)
        si = jnp.concatenate([sim[j, pl.ds(s * pitch, t), :] for j in range(nslab)], axis=1).astype(BF16)
        y = (jnp.dot(sr, wcr_ref[0, s], preferred_element_type=F32)
             - jnp.dot(si, wci_ref[0, s], preferred_element_type=F32))
        o_ref[0, 0, :, s * chb:(s + 1) * chb] = y


def _s5_scan(x, norm_g, mod, lay, t):
    a_re, a_im, we_re, we_im, wc_re, wc_im = lay
    b, l, d = x.shape
    nd = a_re.shape[0]
    chb = d // SSM_BLOCKS
    nsb = a_re.shape[-1]
    nslab = nsb // LANES
    nt = l // t
    pitch = t + 8
    kern = functools.partial(_s5_kernel, t=t, chb=chb, nsb=nsb, pitch=pitch)

    def tile(dd, i):
        return i + dd * (nt - 1 - 2 * i)

    single = pl.Buffered(1)
    return pl.pallas_call(
        kern,
        grid=(nd, b, nt),
        in_specs=[
            pl.BlockSpec((1, t, d), lambda dd, bi, i: (bi, tile(dd, i), 0)),
            pl.BlockSpec((1, d), lambda dd, bi, i: (0, 0)),
            pl.BlockSpec((1, 1, d), lambda dd, bi, i: (bi, 0, 0)),
            pl.BlockSpec((1, 1, d), lambda dd, bi, i: (bi, 0, 1)),
            pl.BlockSpec((1, SSM_BLOCKS, nsb), lambda dd, bi, i: (dd, 0, 0)),
            pl.BlockSpec((1, SSM_BLOCKS, nsb), lambda dd, bi, i: (dd, 0, 0)),
            pl.BlockSpec((1, SSM_BLOCKS, chb, nsb), lambda dd, bi, i: (dd, 0, 0, 0), pipeline_mode=single),
            pl.BlockSpec((1, SSM_BLOCKS, chb, nsb), lambda dd, bi, i: (dd, 0, 0, 0), pipeline_mode=single),
            pl.BlockSpec((1, SSM_BLOCKS, nsb, chb), lambda dd, bi, i: (dd, 0, 0, 0), pipeline_mode=single),
            pl.BlockSpec((1, SSM_BLOCKS, nsb, chb), lambda dd, bi, i: (dd, 0, 0, 0), pipeline_mode=single),
        ],
        out_specs=pl.BlockSpec((1, 1, t, d), lambda dd, bi, i: (dd, bi, tile(dd, i), 0)),
        out_shape=jax.ShapeDtypeStruct((nd, b, l, d), F32),
        scratch_shapes=[
            pltpu.VMEM((nslab, SSM_BLOCKS * pitch, LANES), F32),
            pltpu.VMEM((nslab, SSM_BLOCKS * pitch, LANES), F32),
            pltpu.VMEM((SSM_BLOCKS, nsb), F32),
            pltpu.VMEM((SSM_BLOCKS, nsb), F32),
        ],
        compiler_params=_cparams(("arbitrary", "arbitrary", "arbitrary")),
        name="s5_scan",
    )(x, norm_g.reshape(1, d), mod, mod, a_re, a_im, we_re, we_im, wc_re, wc_im)


def _s5_glu_kernel(yf_ref, yb_ref, xf_ref, g_ref, sh_ref, sc_ref, dsk_ref, wa_ref, wg_ref, ba_ref, bg_ref,
                   x_ref, g1_ref, o_ref, v_scr):
    @pl.when(pl.program_id(2) == 0)
    def _():
        h = _modnorm(xf_ref[0], g_ref[...], sh_ref[0], sc_ref[0])
        y = yf_ref[0, 0] + yb_ref[0, 0] + dsk_ref[...] * h
        v_scr[...] = _gelu_tanh(y).astype(BF16)

    v = v_scr[...]
    a = jnp.dot(v, wa_ref[...], preferred_element_type=F32) + ba_ref[...]
    g = jnp.dot(v, wg_ref[...], preferred_element_type=F32) + bg_ref[...]
    o_ref[0] = x_ref[0] + g1_ref[0] * (a * jax.nn.sigmoid(g))


def _s5_glu(y2, x, norm_g, mod, d_skip, w_glu_bf, b_glu, tm, tn):
    b, l, d = x.shape
    nj = d // tn
    return pl.pallas_call(
        _s5_glu_kernel,
        grid=(b, l // tm, nj),
        in_specs=[
            pl.BlockSpec((1, 1, tm, d), lambda bi, i, j: (0, bi, i, 0)),
            pl.BlockSpec((1, 1, tm, d), lambda bi, i, j: (1, bi, i, 0)),
            pl.BlockSpec((1, tm, d), lambda bi, i, j: (bi, i, 0)),
            pl.BlockSpec((1, d), lambda bi, i, j: (0, 0)),
            pl.BlockSpec((1, 1, d), lambda bi, i, j: (bi, 0, 0)),
            pl.BlockSpec((1, 1, d), lambda bi, i, j: (bi, 0, 1)),
            pl.BlockSpec((1, d), lambda bi, i, j: (0, 0)),
            pl.BlockSpec((d, tn), lambda bi, i, j: (0, j)),
            pl.BlockSpec((d, tn), lambda bi, i, j: (0, j + nj)),
            pl.BlockSpec((1, tn), lambda bi, i, j: (0, j)),
            pl.BlockSpec((1, tn), lambda bi, i, j: (0, j + nj)),
            pl.BlockSpec((1, tm, tn), lambda bi, i, j: (bi, i, j)),
            pl.BlockSpec((1, 1, tn), lambda bi, i, j: (bi, 0, 2 * nj + j)),
        ],
        out_specs=pl.BlockSpec((1, tm, tn), lambda bi, i, j: (bi, i, j)),
        out_shape=jax.ShapeDtypeStruct((b, l, d), F32),
        scratch_shapes=[pltpu.VMEM((tm, d), BF16)],
        compiler_params=_cparams(("parallel", "parallel", "arbitrary")),
        name="s5_gelu_glu",
    )(y2, y2, x, norm_g.reshape(1, d), mod, mod, d_skip.reshape(1, d), w_glu_bf, w_glu_bf,
      b_glu.reshape(1, 2 * d), b_glu.reshape(1, 2 * d), x, mod)


def _router_kernel(x_ref, g_ref, sh_ref, sc_ref, wr_ref, hp_ref, aff_ref, *, dh):
    h = _modnorm(x_ref[0], g_ref[...], sh_ref[0], sc_ref[0])
    bits = lax.bitcast_convert_type(h.astype(BF16).astype(F32), I32)
    lo = lax.shift_right_logical(bits[:, :dh], 16)
    hi = jnp.bitwise_and(bits[:, dh:], jnp.int32(-65536))
    hp_ref[0] = jnp.bitwise_or(hi, lo)
    logits = lax.dot_general(wr_ref[...], h, (((1,), (1,)), ((), ())), precision=HIGHEST,
                             preferred_element_type=F32)
    m = jnp.max(logits, axis=0, keepdims=True)
    ex = jnp.exp(logits - m)
    aff_ref[...] = ex / jnp.sum(ex, axis=0, keepdims=True)


def _router(x, norm_g, mod, w_router, tm):
    b, l, d = x.shape
    e = w_router.shape[1]
    dh = d // 2
    nt = l // tm
    kern = functools.partial(_router_kernel, dh=dh)
    return pl.pallas_call(
        kern,
        grid=(b, nt),
        in_specs=[
            pl.BlockSpec((1, tm, d), lambda bi, i: (bi, i, 0)),
            pl.BlockSpec((1, d), lambda bi, i: (0, 0)),
            pl.BlockSpec((1, 1, d), lambda bi, i: (bi, 0, 3)),
            pl.BlockSpec((1, 1, d), lambda bi, i: (bi, 0, 4)),
            pl.BlockSpec((e, d), lambda bi, i: (0, 0)),
        ],
        out_specs=[
            pl.BlockSpec((1, tm, dh), lambda bi, i: (bi, i, 0)),
            pl.BlockSpec((e, tm), lambda bi, i: (0, bi * nt + i)),
        ],
        out_shape=[jax.ShapeDtypeStruct((b, l, dh), I32), jax.ShapeDtypeStruct((e, b * l), F32)],
        compiler_params=_cparams(("parallel", "parallel")),
        name="moe_router",
    )(x, norm_g.reshape(1, d), mod, mod, w_router.T)


def _excl_cumsum_lanes(x):
    n = x.shape[1]
    lane = lax.broadcasted_iota(I32, x.shape, 1)
    inc = x
    k = 1
    while k < n:
        inc = inc + jnp.where(lane >= k, pltpu.roll(inc, k, axis=1), 0)
        k *= 2
    return inc - x


def _select_kernel(aff_ref, idx_ref, gate_ref, off_ref, pos_scr, *, cap, ntile, tmtok):
    e, n = aff_ref.shape
    key = lax.bitcast_convert_type(aff_ref[...], I32)

    def bit_step(it, prefix):
        cand = jnp.bitwise_or(prefix, jnp.left_shift(jnp.int32(1), 30 - it))
        cnt = jnp.sum((key >= cand).astype(I32), axis=1, keepdims=True)
        return jnp.where(cnt >= cap, cand, prefix)

    thr = lax.fori_loop(0, 31, bit_step, jnp.zeros((e, 1), I32))
    gt = key > thr
    eq = key == thr
    need = cap - jnp.sum(gt.astype(I32), axis=1, keepdims=True)
    sel = jnp.logical_or(gt, jnp.logical_and(eq, _excl_cumsum_lanes(eq.astype(I32)) < need))
    sel_i = sel.astype(I32)
    pos_scr[...] = jnp.where(sel, _excl_cumsum_lanes(sel_i), -1)

    tok = lax.broadcasted_iota(I32, (e, n), 1)
    lane = lax.broadcasted_iota(I32, off_ref.shape, 1)
    off = jnp.zeros(off_ref.shape, I32)
    for b in range(1, ntile + 1):
        cnt = jnp.sum(jnp.where(tok < b * tmtok, sel_i, 0), axis=1, keepdims=True)
        off = jnp.where(lane == b, cnt, off)
    off_ref[...] = off

    tok1 = lax.broadcasted_iota(I32, (8, n), 1)
    row = lax.broadcasted_iota(I32, (8, n), 0)
    tok_hi = jnp.right_shift(tok1, 7).astype(F32)
    tok_lo = jnp.bitwise_and(tok1, 127).astype(F32)

    def expert(ei, _):
        pos_e = pos_scr[pl.ds(ei, 1), :]
        g = aff_ref[pl.ds(ei, 1), :]
        g0 = g.astype(BF16).astype(F32)
        r1 = g - g0
        g1 = r1.astype(BF16).astype(F32)
        g2 = r1 - g1
        vals = jnp.where(row == 0, tok_hi, jnp.where(row == 1, tok_lo, jnp.where(
            row == 2, g0, jnp.where(row == 3, g1, jnp.where(row == 4, g2, 0.0))))).astype(BF16)

        def chunk(ci, _):
            slot = ci * SLOT_BLOCK + lax.broadcasted_iota(I32, (SLOT_BLOCK, n), 0)
            onehot = jnp.where(pos_e == slot, 1.0, 0.0).astype(BF16)
            r = lax.dot_general(vals, onehot, (((1,), (1,)), ((), ())), preferred_element_type=F32)
            idx_ref[ei, pl.ds(ci, 1), :] = (r[0:1] * 128.0 + r[1:2]).astype(I32)
            gate_ref[ei, pl.ds(ci, 1), :] = r[2:3] + r[3:4] + r[4:5]
            return 0

        lax.fori_loop(0, cap // SLOT_BLOCK, chunk, 0)
        return 0

    lax.fori_loop(0, e, expert, 0)


def _select(aff_t, cap, ntile, tmtok):
    e, n = aff_t.shape
    nc = cap // SLOT_BLOCK
    kern = functools.partial(_select_kernel, cap=cap, ntile=ntile, tmtok=tmtok)
    idx, gate, off = pl.pallas_call(
        kern,
        out_shape=[jax.ShapeDtypeStruct((e, nc, SLOT_BLOCK), I32),
                   jax.ShapeDtypeStruct((e, nc, SLOT_BLOCK), F32),
                   jax.ShapeDtypeStruct((e, LANES), I32)],
        scratch_shapes=[pltpu.VMEM((e, n), I32)],
        compiler_params=pltpu.CompilerParams(vmem_limit_bytes=VMEM_LIMIT),
        name="moe_select",
    )(aff_t)
    return idx.reshape(e, cap), gate.reshape(e, cap), off


def _ffn_kernel(idx_sm, hp_hbm, wg_ref, wu_ref, wd_ref, gate_ref, o_ref, xp_scr, xs_scr, hid_scr, sem,
                *, ngrp, cap, nf, tf, dh):
    e = pl.program_id(0)
    s = pl.program_id(1)
    ne = pl.num_programs(0)
    m = ngrp * cap

    def row_copy(grp, src_row, dst_row):
        return pltpu.make_async_copy(hp_hbm.at[grp, pl.ds(src_row, 1), :], xp_scr.at[pl.ds(dst_row, 1), :], sem)

    def issue_gather(expert):
        for grp in range(ngrp):
            base = (grp * ne + expert) * cap

            def body(c, _):
                row_copy(grp, idx_sm[base + c], grp * cap + c).start()
                return 0

            lax.fori_loop(0, cap, body, 0, unroll=8)

    @pl.when(jnp.logical_and(e == 0, s == 0))
    def _():
        issue_gather(0)

    @pl.when(s == 0)
    def _():
        pltpu.make_async_copy(hp_hbm.at[0, pl.ds(0, m), :], xp_scr, sem).wait()
        p = xp_scr[...]
        xs_scr[0] = lax.bitcast_convert_type(jnp.left_shift(p, 16), F32).astype(BF16)
        xs_scr[1] = lax.bitcast_convert_type(jnp.bitwise_and(p, jnp.int32(-65536)), F32).astype(BF16)

        @pl.when(e + 1 < ne)
        def _():
            issue_gather(e + 1)

    @pl.when(s < nf)
    def _():
        wg = wg_ref[0].astype(BF16)
        wu = wu_ref[0].astype(BF16)
        xl = xs_scr[0]
        xh = xs_scr[1]
        g = (jnp.dot(xl, wg[:dh], preferred_element_type=F32) + jnp.dot(xh, wg[dh:], preferred_element_type=F32))
        u = (jnp.dot(xl, wu[:dh], preferred_element_type=F32) + jnp.dot(xh, wu[dh:], preferred_element_type=F32))
        hid_scr[s] = (_silu(g) * u).astype(BF16)

    @pl.when(s >= nf)
    def _():
        wd = wd_ref[0].astype(BF16)
        acc = jnp.dot(hid_scr[0], wd[:tf], preferred_element_type=F32)
        for k in range(1, nf):
            acc = acc + jnp.dot(hid_scr[k], wd[k * tf:(k + 1) * tf], preferred_element_type=F32)
        for grp in range(ngrp):
            o_ref[grp, 0] = acc[grp * cap:(grp + 1) * cap] * gate_ref[grp, 0]


def _ffn(idx_flat, hp, w_gate, w_up, w_down, gate_col, tf, tn):
    ngrp, n, dh = hp.shape
    ne, d, f = w_gate.shape
    cap = gate_col.shape[2]
    nf = f // tf
    nn = d // tn
    m = ngrp * cap
    kern = functools.partial(_ffn_kernel, ngrp=ngrp, cap=cap, nf=nf, tf=tf, dh=dh)
    grid_spec = pltpu.PrefetchScalarGridSpec(
        num_scalar_prefetch=1,
        grid=(ne, nf + nn),
        in_specs=[
            pl.BlockSpec(memory_space=pl.ANY),
            pl.BlockSpec((1, d, tf), lambda e, s, idx: (e, 0, jnp.minimum(s, nf - 1))),
            pl.BlockSpec((1, d, tf), lambda e, s, idx: (e, 0, jnp.minimum(s, nf - 1))),
            pl.BlockSpec((1, f, tn), lambda e, s, idx: (e, 0, jnp.maximum(s - nf, 0))),
            pl.BlockSpec((ngrp, 1, cap, 1), lambda e, s, idx: (0, e, 0, 0)),
        ],
        out_specs=pl.BlockSpec((ngrp, 1, cap, tn), lambda e, s, idx: (0, e, 0, jnp.maximum(s - nf, 0))),
        scratch_shapes=[
            pltpu.VMEM((m, dh), I32),
            pltpu.VMEM((2, m, dh), BF16),
            pltpu.VMEM((nf, m, tf), BF16),
            pltpu.SemaphoreType.DMA(()),
        ],
    )
    return pl.pallas_call(
        kern,
        grid_spec=grid_spec,
        out_shape=jax.ShapeDtypeStruct((ngrp, ne, cap, d), F32),
        compiler_params=_cparams(("arbitrary", "arbitrary")),
        name="moe_ffn",
    )(idx_flat, hp, w_gate, w_up, w_down, gate_col)


def _worklist(off, ne, ntile, nblk):
    bs = SLOT_BLOCK
    w_max = ne * (nblk + ntile - 1) + ntile
    lo = off[:, :ntile].T.reshape(-1)
    hi = off[:, 1:ntile + 1].T.reshape(-1)
    b0 = jnp.minimum(lo // bs, nblk - 1)
    nb = jnp.where(hi > lo, (hi + bs - 1) // bs - b0, 0)
    first_e = (jnp.arange(ntile * ne, dtype=I32) % ne) == 0
    nb = jnp.where(first_e, jnp.maximum(nb, 1), nb)
    end = jnp.cumsum(nb)
    total = end[-1]
    w = jnp.arange(w_max, dtype=I32)
    wc = jnp.minimum(w, total - 1)
    pair = jnp.searchsorted(end, wc, side='right').astype(I32)
    k = wc - (end[pair] - nb[pair])
    tile = pair // ne
    exp = pair % ne
    blk = b0[pair] + k
    valid = w < total
    rlo = jnp.where(valid, jnp.clip(lo[pair] - blk * bs, 0, bs), 0)
    rhi = jnp.where(valid, jnp.clip(hi[pair] - blk * bs, 0, bs), 0)
    first = jnp.concatenate([jnp.ones((1,), bool), tile[1:] != tile[:-1]])
    last = jnp.concatenate([tile[1:] != tile[:-1], jnp.ones((1,), bool)])
    to_i = lambda a: a.astype(I32)
    return to_i(tile), to_i(exp), to_i(blk), to_i(rlo), to_i(rhi), to_i(first), to_i(last)


def _combine_kernel(tile_sm, exp_sm, blk_sm, rlo_sm, rhi_sm, first_sm, last_sm, idx_sm,
                    src_ref, x_ref, g2_ref, fg_ref, o_ref, acc_scr, *, cap, tmtok, final):
    w = pl.program_id(0)

    @pl.when(first_sm[w] == 1)
    def _():
        acc_scr[...] = jnp.zeros_like(acc_scr)

    slot0 = exp_sm[w] * cap + blk_sm[w] * SLOT_BLOCK
    tok0 = tile_sm[w] * tmtok

    def row(r, _):
        tok = idx_sm[slot0 + r] - tok0
        acc_scr[pl.ds(tok, 1), :] = acc_scr[pl.ds(tok, 1), :] + src_ref[0, 0, pl.ds(r, 1), :]
        return 0

    lax.fori_loop(rlo_sm[w], rhi_sm[w], row, 0)

    @pl.when(last_sm[w] == 1)
    def _():
        xn = x_ref[0] + g2_ref[0] * acc_scr[...]
        if final:
            ms = jnp.mean(xn * xn, axis=-1, keepdims=True)
            xn = xn * lax.rsqrt(ms + RMS_EPS) * fg_ref[...]
        o_ref[0] = xn


def _combine(work, idx_flat_g, outs, grp, x, mod, final_g, tmtok, final):
    b, l, d = x.shape
    ne, cap = outs.shape[1], outs.shape[2]
    tpb = l // tmtok
    w_max = work[0].shape[0]
    kern = functools.partial(_combine_kernel, cap=cap, tmtok=tmtok, final=final)
    nj = 1
    grid_spec = pltpu.PrefetchScalarGridSpec(
        num_scalar_prefetch=8,
        grid=(w_max,),
        in_specs=[
            pl.BlockSpec((1, 1, SLOT_BLOCK, d),
                         lambda w, ti, ex, bl, *_: (grp, ex[w], bl[w], 0)),
            pl.BlockSpec((1, tmtok, d), lambda w, ti, *_: (ti[w] // tpb, ti[w] % tpb, 0),
                         pipeline_mode=pl.Buffered(1)),
            pl.BlockSpec((1, 1, d), lambda w, ti, *_: (ti[w] // tpb, 0, 5 * nj)),
            pl.BlockSpec((1, d), lambda w, *_: (0, 0)),
        ],
        out_specs=pl.BlockSpec((1, tmtok, d), lambda w, ti, *_: (ti[w] // tpb, ti[w] % tpb, 0)),
        scratch_shapes=[pltpu.VMEM((tmtok, d), F32)],
    )
    return pl.pallas_call(
        kern,
        grid_spec=grid_spec,
        out_shape=jax.ShapeDtypeStruct((b, l, d), F32),
        compiler_params=_cparams(("arbitrary",)),
        name="moe_combine",
    )(*work, idx_flat_g, outs, x, mod, final_g.reshape(1, d))


def _pick(n, pref):
    t = min(pref, n)
    while n % t:
        t //= 2
    return t


def _moe_layer(xs, mods, norm_g, w_router, w_gate, w_up, w_down, final_g, final):
    ne = w_router.shape[1]
    hps, idxs, gates, works = [], [], [], []
    for x, mod in zip(xs, mods):
        b, l, d = x.shape
        n = b * l
        cap = EC_CAPACITY_FACTOR * n // ne
        tmtok = _pick(l, 1024)
        ntile = n // tmtok
        hp, aff_t = _router(x, norm_g, mod, w_router, _pick(l, 512))
        idx, gate, off = _select(aff_t, cap, ntile, tmtok)
        hps.append(hp.reshape(n, d // 2))
        idxs.append(idx)
        gates.append(gate)
        works.append(_worklist(off, ne, ntile, cap // SLOT_BLOCK))
    hp_all = jnp.stack(hps)
    idx_all = jnp.stack(idxs)
    gate_col = jnp.stack(gates)[..., None]
    f = w_gate.shape[2]
    outs = _ffn(idx_all.reshape(-1), hp_all, w_gate, w_up, w_down, gate_col, _pick(f, 256), _pick(xs[0].shape[2], 256))
    new = []
    for grp, (x, mod) in enumerate(zip(xs, mods)):
        tmtok = _pick(x.shape[1], 1024)
        new.append(_combine(works[grp], idxs[grp].reshape(-1), outs, grp, x, mod, final_g, tmtok, final))
    return new


def kernel(x_prompt, x_sample, c_prompt, c_sample, ada_w, ada_b, norm_mix_g, norm_ffn_g, final_norm_g, conv_w_in, conv_b_in, conv_w_dw, conv_b_dw, conv_ln_g, conv_ln_b, conv_w_out, conv_b_out, ssm_lambda_re, ssm_lambda_im, ssm_log_step, ssm_b_re, ssm_b_im, ssm_c_re, ssm_c_im, ssm_d, ssm_w_glu, ssm_b_glu, moe_w_router, moe_w_gate, moe_w_up, moe_w_down):
    depth = ada_w.shape[0]
    d = x_prompt.shape[-1]
    bp, bs = x_prompt.shape[0], x_sample.shape[0]
    nrow = 8 * ((bp + bs + 7) // 8)
    c_all = jnp.concatenate([c_prompt, c_sample, jnp.zeros((nrow - bp - bs, d), F32)], axis=0)
    mod_all = _ada(c_all, ada_w, ada_b)

    xs = [x_prompt, x_sample]
    for i in range(depth):
        mods = [mod_all[i, :bp].reshape(bp, 1, 6 * d), mod_all[i, bp:bp + bs].reshape(bs, 1, 6 * d)]
        j = i // 2
        if i % 2 == 0:
            w_in = conv_w_in[j].astype(BF16)
            w_out = conv_w_out[j].astype(BF16)
            nxt = []
            for x, mod in zip(xs, mods):
                l = x.shape[1]
                u = _conv_in(x, norm_mix_g[i], mod, w_in, conv_b_in[j], _pick(l, 512), _pick(d, 512))
                nxt.append(_conv_out(u, x, mod, conv_w_dw[j], conv_b_dw[j], conv_ln_g[j], conv_ln_b[j],
                                     w_out, conv_b_out[j], _pick(l, 512), _pick(d, 512)))
            xs = nxt
        else:
            lay = _s5_layout(ssm_lambda_re[j], ssm_lambda_im[j], ssm_log_step[j], ssm_b_re[j], ssm_b_im[j],
                             ssm_c_re[j], ssm_c_im[j])
            w_glu = ssm_w_glu[j].astype(BF16)
            nxt = []
            for x, mod in zip(xs, mods):
                l = x.shape[1]
                y2 = _s5_scan(x, norm_mix_g[i], mod, lay, _pick(l, 256))
                nxt.append(_s5_glu(y2, x, norm_mix_g[i], mod, ssm_d[j], w_glu, ssm_b_glu[j],
                                   _pick(l, 512), _pick(d, 512)))
            xs = nxt
        xs = _moe_layer(xs, mods, norm_ffn_g[i], moe_w_router[i], moe_w_gate[i], moe_w_up[i], moe_w_down[i],
                        final_norm_g, final=(i == depth - 1))
    return (xs[0], xs[1])
```

```python
import functools
import math

import jax
import jax.numpy as jnp
from jax import lax
from jax.experimental import pallas as pl
from jax.experimental.pallas import tpu as pltpu

F32 = jnp.float32
BF16 = jnp.bfloat16
I32 = jnp.int32
HIGHEST = lax.Precision.HIGHEST

RMS_EPS = 1e-6
LN_EPS = 1e-5
CONV_WIDTH = 31
CONV_PAD = (CONV_WIDTH - 1) // 2
CONV_HALO = 16
SSM_GROUP = 16
SSM_STATE = 64
SSM_BLOCKS = 8
S5_LAG = 4
N_EXPERTS = 16
EC_CAPACITY_FACTOR = 2
LANES = 128
SLOT_BLOCK = 128
ROW_CHUNK = 16

VMEM_LIMIT = 56 * 1024 * 1024


def _cparams(sem, vmem=VMEM_LIMIT):
    return pltpu.CompilerParams(dimension_semantics=sem, vmem_limit_bytes=vmem)


def _modnorm(x, g, sh, sc):
    ms = jnp.mean(x * x, axis=-1, keepdims=True)
    y = x * lax.rsqrt(ms + RMS_EPS) * g
    return y * (1.0 + sc) + sh


def _silu(x):
    return x * jax.nn.sigmoid(x)


def _gelu_tanh(x):
    c = math.sqrt(2.0 / math.pi)
    return 0.5 * x * (1.0 + jnp.tanh(c * (x + 0.044715 * (x * x * x))))


def _ada_kernel(c_ref, w_ref, b_ref, o_ref):
    c = c_ref[...]
    cond = _silu(c)
    o_ref[0] = jnp.dot(cond, w_ref[0], precision=HIGHEST, preferred_element_type=F32) + b_ref[0]


def _ada(c_all, ada_w, ada_b):
    depth, d, n6 = ada_w.shape
    rows = c_all.shape[0]
    tn = _pick(n6, 1024)
    return pl.pallas_call(
        _ada_kernel,
        grid=(depth, n6 // tn),
        in_specs=[
            pl.BlockSpec((rows, d), lambda i, j: (0, 0)),
            pl.BlockSpec((1, d, tn), lambda i, j: (i, 0, j)),
            pl.BlockSpec((1, 1, tn), lambda i, j: (i, 0, j)),
        ],
        out_specs=pl.BlockSpec((1, rows, tn), lambda i, j: (i, 0, j)),
        out_shape=jax.ShapeDtypeStruct((depth, rows, n6), F32),
        compiler_params=_cparams(("parallel", "parallel")),
        name="ada_mod",
    )(c_all, ada_w, ada_b.reshape(depth, 1, n6))


def _conv_in_kernel(x_ref, g_ref, sh_ref, sc_ref, wa_ref, wg_ref, ba_ref, bg_ref, o_ref, h_scr):
    @pl.when(pl.program_id(2) == 0)
    def _():
        def chunk(r, _):
            rows = pl.ds(pl.multiple_of(r * ROW_CHUNK, ROW_CHUNK), ROW_CHUNK)
            h_scr[rows, :] = _modnorm(x_ref[0, rows, :], g_ref[...], sh_ref[0], sc_ref[0]).astype(BF16)
            return 0

        lax.fori_loop(0, h_scr.shape[0] // ROW_CHUNK, chunk, 0)

    h = h_scr[...]
    a = jnp.dot(h, wa_ref[...], preferred_element_type=F32) + ba_ref[...]
    g = jnp.dot(h, wg_ref[...], preferred_element_type=F32) + bg_ref[...]
    o_ref[0] = a * jax.nn.sigmoid(g)


def _conv_in(x, norm_g, mod, w_in_bf, b_in, tm, tn):
    b, l, d = x.shape
    nj = d // tn
    return pl.pallas_call(
        _conv_in_kernel,
        grid=(b, l // tm, nj),
        in_specs=[
            pl.BlockSpec((1, tm, d), lambda bi, i, j: (bi, i, 0)),
            pl.BlockSpec((1, d), lambda bi, i, j: (0, 0)),
            pl.BlockSpec((1, 1, d), lambda bi, i, j: (bi, 0, 0)),
            pl.BlockSpec((1, 1, d), lambda bi, i, j: (bi, 0, 1)),
            pl.BlockSpec((d, tn), lambda bi, i, j: (0, j)),
            pl.BlockSpec((d, tn), lambda bi, i, j: (0, j + nj)),
            pl.BlockSpec((1, tn), lambda bi, i, j: (0, j)),
            pl.BlockSpec((1, tn), lambda bi, i, j: (0, j + nj)),
        ],
        out_specs=pl.BlockSpec((1, tm, tn), lambda bi, i, j: (bi, i, j)),
        out_shape=jax.ShapeDtypeStruct((b, l, d), F32),
        scratch_shapes=[pltpu.VMEM((tm, d), BF16)],
        compiler_params=_cparams(("parallel", "parallel", "arbitrary")),
        name="conv_in_glu",
    )(x, norm_g.reshape(1, d), mod, mod, w_in_bf, w_in_bf, b_in.reshape(1, 2 * d), b_in.reshape(1, 2 * d))


def _conv_out_kernel(u_ref, up_ref, un_ref, wdw_ref, bdw_ref, lg_ref, lb_ref, wo_ref, bo_ref, x_ref, g1_ref,
                     o_ref, ext_scr, c_scr, v_scr, *, tm, d, rc, cc):
    i = pl.program_id(1)
    ni = pl.num_programs(1)

    ncol = d // cc

    @pl.when(pl.program_id(2) == 0)
    def _():
        for c in range(ncol):
            cols = pl.ds(c * cc, cc)
            ext_scr[c, pl.ds(0, CONV_HALO), :] = jnp.where(i > 0, up_ref[0, :, cols], 0.0)
            ext_scr[c, pl.ds(CONV_HALO, tm), :] = u_ref[0, :, cols]
            ext_scr[c, pl.ds(CONV_HALO + tm, CONV_HALO), :] = jnp.where(i < ni - 1, un_ref[0, :, cols], 0.0)

        def col_chunk(c, _):
            for r0 in range(0, tm, rc):
                acc = jnp.zeros((rc, cc), F32) + bdw_ref[c]
                for k in range(CONV_WIDTH):
                    seg = ext_scr.at[c][pl.ds(r0 + (CONV_HALO - CONV_PAD + k), rc, stride=1), :]
                    acc = acc + seg * wdw_ref[c, pl.ds(k, 1), :]
                c_scr[c, pl.ds(r0, rc), :] = acc
            return 0

        lax.fori_loop(0, ncol, col_chunk, 0)

        def ln_chunk(r, _):
            r0 = pl.multiple_of(r * rc, rc)
            xc = [c_scr[c, pl.ds(r0, rc), :] for c in range(ncol)]
            mu = sum(jnp.sum(v, axis=-1, keepdims=True) for v in xc) * (1.0 / d)
            xm = [v - mu for v in xc]
            var = sum(jnp.sum(v * v, axis=-1, keepdims=True) for v in xm) * (1.0 / d)
            inv = lax.rsqrt(var + LN_EPS)
            for c in range(ncol):
                cols = pl.ds(c * cc, cc)
                y = xm[c] * inv * lg_ref[:, cols] + lb_ref[:, cols]
                v_scr[pl.ds(r0, rc), cols] = _silu(y).astype(BF16)
            return 0

        lax.fori_loop(0, tm // rc, ln_chunk, 0)

    o = jnp.dot(v_scr[...], wo_ref[...], preferred_element_type=F32) + bo_ref[...]
    o_ref[0] = x_ref[0] + g1_ref[0] * o


def _conv_out(u, x, mod, w_dw, b_dw, ln_g, ln_b, w_out_bf, b_out, tm, tn):
    b, l, d = x.shape
    nj = d // tn
    hb = tm // CONV_HALO
    nhb = l // CONV_HALO
    rc = min(64, tm)
    cc = min(LANES, d)
    ncol = d // cc
    w_dw_p = jnp.concatenate([w_dw, jnp.zeros((1, d), w_dw.dtype)], axis=0)
    w_dw_p = w_dw_p.reshape(CONV_WIDTH + 1, ncol, cc).transpose(1, 0, 2)
    kern = functools.partial(_conv_out_kernel, tm=tm, d=d, rc=rc, cc=cc)
    return pl.pallas_call(
        kern,
        grid=(b, l // tm, nj),
        in_specs=[
            pl.BlockSpec((1, tm, d), lambda bi, i, j: (bi, i, 0)),
            pl.BlockSpec((1, CONV_HALO, d), lambda bi, i, j: (bi, jnp.maximum(i * hb - 1, 0), 0)),
            pl.BlockSpec((1, CONV_HALO, d), lambda bi, i, j: (bi, jnp.minimum((i + 1) * hb, nhb - 1), 0)),
            pl.BlockSpec((ncol, CONV_WIDTH + 1, cc), lambda bi, i, j: (0, 0, 0)),
            pl.BlockSpec((ncol, 1, cc), lambda bi, i, j: (0, 0, 0)),
            pl.BlockSpec((1, d), lambda bi, i, j: (0, 0)),
            pl.BlockSpec((1, d), lambda bi, i, j: (0, 0)),
            pl.BlockSpec((d, tn), lambda bi, i, j: (0, j)),
            pl.BlockSpec((1, tn), lambda bi, i, j: (0, j)),
            pl.BlockSpec((1, tm, tn), lambda bi, i, j: (bi, i, j)),
            pl.BlockSpec((1, 1, tn), lambda bi, i, j: (bi, 0, 2 * nj + j)),
        ],
        out_specs=pl.BlockSpec((1, tm, tn), lambda bi, i, j: (bi, i, j)),
        out_shape=jax.ShapeDtypeStruct((b, l, d), F32),
        scratch_shapes=[
            pltpu.VMEM((ncol, tm + 2 * CONV_HALO, cc), F32),
            pltpu.VMEM((ncol, tm, cc), F32),
            pltpu.VMEM((tm, d), BF16),
        ],
        compiler_params=_cparams(("parallel", "parallel", "arbitrary")),
        name="conv_dw_ln_out",
    )(u, u, u, w_dw_p, b_dw.reshape(ncol, 1, cc), ln_g.reshape(1, d), ln_b.reshape(1, d), w_out_bf,
      b_out.reshape(1, d), x, mod)


def _zoh_discretise(lam_re, lam_im, log_step, b_re, b_im):
    dt = jnp.exp(log_step)[..., None]
    mag = jnp.exp(lam_re * dt)
    a_re = mag * jnp.cos(lam_im * dt)
    a_im = mag * jnp.sin(lam_im * dt)
    nr = a_re - 1.0
    ni = a_im
    den = lam_re * lam_re + lam_im * lam_im
    k_re = (nr * lam_re + ni * lam_im) / den
    k_im = (ni * lam_re - nr * lam_im) / den
    bb_re = k_re[..., None] * b_re - k_im[..., None] * b_im
    bb_im = k_re[..., None] * b_im + k_im[..., None] * b_re
    return a_re, a_im, bb_re, bb_im


def _s5_layout(lam_re, lam_im, log_step, b_re, b_im, c_re, c_im):
    a_re, a_im, bb_re, bb_im = _zoh_discretise(lam_re, lam_im, log_step, b_re, b_im)
    nd, g, p = a_re.shape
    c = b_re.shape[-1]
    gpb = g // SSM_BLOCKS
    eye = jnp.eye(gpb, dtype=F32)

    def expand(bb):
        bb = bb.reshape(nd, SSM_BLOCKS, gpb, p, c)
        return jnp.einsum('dsgpc,gh->dsgchp', bb, eye).reshape(nd, SSM_BLOCKS, gpb * c, gpb * p).astype(BF16)

    def contract(cc):
        cc = cc.reshape(nd, SSM_BLOCKS, gpb, c, p)
        return jnp.einsum('dsgcp,gh->dshpgc', cc, eye).reshape(nd, SSM_BLOCKS, gpb * p, gpb * c).astype(BF16)

    return (a_re.reshape(nd, SSM_BLOCKS, gpb * p), a_im.reshape(nd, SSM_BLOCKS, gpb * p),
            expand(bb_re), expand(bb_im), contract(c_re), contract(c_im))


def _s5_kernel(x_ref, g_ref, sh_ref, sc_ref, ar_ref, ai_ref, wer_ref, wei_ref, wcr_ref, wci_ref,
               o_ref, sre, sim, cre, cim, h_scr, *, t, chb, nsb, pitch):
    dirn = pl.program_id(0)
    nslab = nsb // LANES

    @pl.when(jnp.logical_and(jnp.logical_and(dirn == 0, pl.program_id(1) == 0), pl.program_id(2) == 0))
    def _():
        sre[...] = jnp.zeros_like(sre)
        sim[...] = jnp.zeros_like(sim)

    @pl.when(pl.program_id(2) == 0)
    def _():
        cre[...] = jnp.zeros_like(cre)
        cim[...] = jnp.zeros_like(cim)

    def norm_chunk(r, _):
        rows = pl.ds(pl.multiple_of(r * ROW_CHUNK, ROW_CHUNK), ROW_CHUNK)
        h_scr[rows, :] = _modnorm(x_ref[0, rows, :], g_ref[...], sh_ref[0], sc_ref[0]).astype(BF16)
        return 0

    lax.fori_loop(0, t // ROW_CHUNK, norm_chunk, 0)

    def base(s):
        return s * pitch + (S5_LAG if s % 2 else 0)

    for s in range(SSM_BLOCKS):
        hs = h_scr[:, s * chb:(s + 1) * chb]
        bur = jnp.dot(hs, wer_ref[0, s], preferred_element_type=F32)
        bui = jnp.dot(hs, wei_ref[0, s], preferred_element_type=F32)
        for j in range(nslab):
            sre[j, pl.ds(base(s), t), :] = bur[:, j * LANES:(j + 1) * LANES]
            sim[j, pl.ds(base(s), t), :] = bui[:, j * LANES:(j + 1) * LANES]

    ar = [ar_ref[0, :, j * LANES:(j + 1) * LANES] for j in range(nslab)]
    ai = [ai_ref[0, :, j * LANES:(j + 1) * LANES] for j in range(nslab)]
    odd = jnp.bitwise_and(lax.broadcasted_iota(I32, (SSM_BLOCKS, LANES), 0), 1)
    nrow = t + S5_LAG

    def make_step(active):
        def step(k, carry):
            xr, xi = carry
            row = jnp.where(dirn == 0, k, nrow - 1 - k)
            rows = pl.ds(row, SSM_BLOCKS, stride=pitch)
            nr, ni = [], []
            for j in range(nslab):
                br = sre.at[j][rows, :]
                bi = sim.at[j][rows, :]
                r = ar[j] * xr[j] - ai[j] * xi[j] + br
                im = ar[j] * xi[j] + ai[j] * xr[j] + bi
                sre.at[j][rows, :] = r
                sim.at[j][rows, :] = im
                if active is not None:
                    r = jnp.where(active, r, xr[j])
                    im = jnp.where(active, im, xi[j])
                nr.append(r)
                ni.append(im)
            return tuple(nr), tuple(ni)
        return step

    x0 = (tuple(cre[:, j * LANES:(j + 1) * LANES] for j in range(nslab)),
          tuple(cim[:, j * LANES:(j + 1) * LANES] for j in range(nslab)))
    x1 = lax.fori_loop(0, S5_LAG, make_step(odd == dirn), x0)
    x2 = lax.fori_loop(S5_LAG, t, make_step(None), x1, unroll=2)
    xr, xi = lax.fori_loop(t, nrow, make_step(odd != dirn), x2)
    for j in range(nslab):
        cre[:, j * LANES:(j + 1) * LANES] = xr[j]
        cim[:, j * LANES:(j + 1) * LANES] = xi[j]

    for s in range(SSM_BLOCKS):
        sr = jnp.concatenate([sre[j, pl.ds(base(s), t), :] for j in range(nslab)], axis=1).astype(BF16)
        si = jnp.concatenate([sim[j, pl.ds(base(s), t), :] for j in range(nslab)], axis=1).astype(BF16)
        y = (jnp.dot(sr, wcr_ref[0, s], preferred_element_type=F32)
             - jnp.dot(si, wci_ref[0, s], preferred_element_type=F32))
        o_ref[0, 0, :, s * chb:(s + 1) * chb] = y


def _s5_scan(x, norm_g, mod, lay, t):
    a_re, a_im, we_re, we_im, wc_re, wc_im = lay
    b, l, d = x.shape
    nd = a_re.shape[0]
    chb = d // SSM_BLOCKS
    nsb = a_re.shape[-1]
    nslab = nsb // LANES
    nt = l // t
    pitch = t + S5_LAG
    kern = functools.partial(_s5_kernel, t=t, chb=chb, nsb=nsb, pitch=pitch)

    def tile(dd, i):
        return i + dd * (nt - 1 - 2 * i)

    single = pl.Buffered(1)
    return pl.pallas_call(
        kern,
        grid=(nd, b, nt),
        in_specs=[
            pl.BlockSpec((1, t, d), lambda dd, bi, i: (bi, tile(dd, i), 0)),
            pl.BlockSpec((1, d), lambda dd, bi, i: (0, 0)),
            pl.BlockSpec((1, 1, d), lambda dd, bi, i: (bi, 0, 0)),
            pl.BlockSpec((1, 1, d), lambda dd, bi, i: (bi, 0, 1)),
            pl.BlockSpec((1, SSM_BLOCKS, nsb), lambda dd, bi, i: (dd, 0, 0)),
            pl.BlockSpec((1, SSM_BLOCKS, nsb), lambda dd, bi, i: (dd, 0, 0)),
            pl.BlockSpec((1, SSM_BLOCKS, chb, nsb), lambda dd, bi, i: (dd, 0, 0, 0), pipeline_mode=single),
            pl.BlockSpec((1, SSM_BLOCKS, chb, nsb), lambda dd, bi, i: (dd, 0, 0, 0), pipeline_mode=single),
            pl.BlockSpec((1, SSM_BLOCKS, nsb, chb), lambda dd, bi, i: (dd, 0, 0, 0), pipeline_mode=single),
            pl.BlockSpec((1, SSM_BLOCKS, nsb, chb), lambda dd, bi, i: (dd, 0, 0, 0), pipeline_mode=single),
        ],
        out_specs=pl.BlockSpec((1, 1, t, d), lambda dd, bi, i: (dd, bi, tile(dd, i), 0)),
        out_shape=jax.ShapeDtypeStruct((nd, b, l, d), F32),
        scratch_shapes=[
            pltpu.VMEM((nslab, SSM_BLOCKS * pitch, LANES), F32),
            pltpu.VMEM((nslab, SSM_BLOCKS * pitch, LANES), F32),
            pltpu.VMEM((SSM_BLOCKS, nsb), F32),
            pltpu.VMEM((SSM_BLOCKS, nsb), F32),
            pltpu.VMEM((t, d), BF16),
        ],
        compiler_params=_cparams(("arbitrary", "arbitrary", "arbitrary")),
        name="s5_scan",
    )(x, norm_g.reshape(1, d), mod, mod, a_re, a_im, we_re, we_im, wc_re, wc_im)


def _s5_glu_kernel(yf_ref, yb_ref, xf_ref, g_ref, sh_ref, sc_ref, dsk_ref, wa_ref, wg_ref, ba_ref, bg_ref,
                   x_ref, g1_ref, o_ref, v_scr):
    @pl.when(pl.program_id(2) == 0)
    def _():
        def chunk(r, _):
            rows = pl.ds(pl.multiple_of(r * ROW_CHUNK, ROW_CHUNK), ROW_CHUNK)
            h = _modnorm(xf_ref[0, rows, :], g_ref[...], sh_ref[0], sc_ref[0])
            y = yf_ref[0, 0, rows, :] + yb_ref[0, 0, rows, :] + dsk_ref[...] * h
            v_scr[rows, :] = _gelu_tanh(y).astype(BF16)
            return 0

        lax.fori_loop(0, v_scr.shape[0] // ROW_CHUNK, chunk, 0)

    v = v_scr[...]
    a = jnp.dot(v, wa_ref[...], preferred_element_type=F32) + ba_ref[...]
    g = jnp.dot(v, wg_ref[...], preferred_element_type=F32) + bg_ref[...]
    o_ref[0] = x_ref[0] + g1_ref[0] * (a * jax.nn.sigmoid(g))


def _s5_glu(y2, x, norm_g, mod, d_skip, w_glu_bf, b_glu, tm, tn):
    b, l, d = x.shape
    nj = d // tn
    return pl.pallas_call(
        _s5_glu_kernel,
        grid=(b, l // tm, nj),
        in_specs=[
            pl.BlockSpec((1, 1, tm, d), lambda bi, i, j: (0, bi, i, 0)),
            pl.BlockSpec((1, 1, tm, d), lambda bi, i, j: (1, bi, i, 0)),
            pl.BlockSpec((1, tm, d), lambda bi, i, j: (bi, i, 0)),
            pl.BlockSpec((1, d), lambda bi, i, j: (0, 0)),
            pl.BlockSpec((1, 1, d), lambda bi, i, j: (bi, 0, 0)),
            pl.BlockSpec((1, 1, d), lambda bi, i, j: (bi, 0, 1)),
            pl.BlockSpec((1, d), lambda bi, i, j: (0, 0)),
            pl.BlockSpec((d, tn), lambda bi, i, j: (0, j)),
            pl.BlockSpec((d, tn), lambda bi, i, j: (0, j + nj)),
            pl.BlockSpec((1, tn), lambda bi, i, j: (0, j)),
            pl.BlockSpec((1, tn), lambda bi, i, j: (0, j + nj)),
            pl.BlockSpec((1, tm, tn), lambda bi, i, j: (bi, i, j)),
            pl.BlockSpec((1, 1, tn), lambda bi, i, j: (bi, 0, 2 * nj + j)),
        ],
        out_specs=pl.BlockSpec((1, tm, tn), lambda bi, i, j: (bi, i, j)),
        out_shape=jax.ShapeDtypeStruct((b, l, d), F32),
        scratch_shapes=[pltpu.VMEM((tm, d), BF16)],
        compiler_params=_cparams(("parallel", "parallel", "arbitrary")),
        name="s5_gelu_glu",
    )(y2, y2, x, norm_g.reshape(1, d), mod, mod, d_skip.reshape(1, d), w_glu_bf, w_glu_bf,
      b_glu.reshape(1, 2 * d), b_glu.reshape(1, 2 * d), x, mod)


def _router_kernel(x_ref, g_ref, sh_ref, sc_ref, wr_ref, hp_ref, aff_ref, hhi_scr, hlo_scr, *, dh):
    def chunk(r, _):
        rows = pl.ds(pl.multiple_of(r * ROW_CHUNK, ROW_CHUNK), ROW_CHUNK)
        h = _modnorm(x_ref[0, rows, :], g_ref[...], sh_ref[0], sc_ref[0])
        hb = h.astype(BF16)
        hbf = hb.astype(F32)
        hhi_scr[rows, :] = hb
        hlo_scr[rows, :] = (h - hbf).astype(BF16)
        bits = lax.bitcast_convert_type(hbf, I32)
        lo = lax.shift_right_logical(bits[:, :dh], 16)
        hi = jnp.bitwise_and(bits[:, dh:], jnp.int32(-65536))
        hp_ref[0, rows, :] = jnp.bitwise_or(hi, lo)
        return 0

    lax.fori_loop(0, hhi_scr.shape[0] // ROW_CHUNK, chunk, 0)

    w = wr_ref[...]
    w_hi = w.astype(BF16)
    w_lo = (w - w_hi.astype(F32)).astype(BF16)
    nt = (((1,), (1,)), ((), ()))
    h_hi = hhi_scr[...]
    logits = (lax.dot_general(w_hi, h_hi, nt, preferred_element_type=F32)
              + lax.dot_general(w_lo, h_hi, nt, preferred_element_type=F32)
              + lax.dot_general(w_hi, hlo_scr[...], nt, preferred_element_type=F32))
    m = jnp.max(logits, axis=0, keepdims=True)
    ex = jnp.exp(logits - m)
    aff_ref[...] = ex / jnp.sum(ex, axis=0, keepdims=True)


def _router(x, norm_g, mod, w_router, tm):
    b, l, d = x.shape
    e = w_router.shape[1]
    dh = d // 2
    nt = l // tm
    kern = functools.partial(_router_kernel, dh=dh)
    return pl.pallas_call(
        kern,
        grid=(b, nt),
        in_specs=[
            pl.BlockSpec((1, tm, d), lambda bi, i: (bi, i, 0)),
            pl.BlockSpec((1, d), lambda bi, i: (0, 0)),
            pl.BlockSpec((1, 1, d), lambda bi, i: (bi, 0, 3)),
            pl.BlockSpec((1, 1, d), lambda bi, i: (bi, 0, 4)),
            pl.BlockSpec((e, d), lambda bi, i: (0, 0)),
        ],
        out_specs=[
            pl.BlockSpec((1, tm, dh), lambda bi, i: (bi, i, 0)),
            pl.BlockSpec((e, tm), lambda bi, i: (0, bi * nt + i)),
        ],
        out_shape=[jax.ShapeDtypeStruct((b, l, dh), I32), jax.ShapeDtypeStruct((e, b * l), F32)],
        scratch_shapes=[pltpu.VMEM((tm, d), BF16), pltpu.VMEM((tm, d), BF16)],
        compiler_params=_cparams(("parallel", "parallel")),
        name="moe_router",
    )(x, norm_g.reshape(1, d), mod, mod, w_router.T)


def _excl_cumsum_lanes(x):
    n = x.shape[1]
    lane = lax.broadcasted_iota(I32, x.shape, 1)
    inc = x
    k = 1
    while k < n:
        inc = inc + jnp.where(lane >= k, pltpu.roll(inc, k, axis=1), 0)
        k *= 2
    return inc - x


def _select_kernel(aff_ref, idx_ref, gate_ref, off_ref, pos_scr, *, cap, ntile, tmtok):
    e, n = aff_ref.shape
    key = lax.bitcast_convert_type(aff_ref[...], I32)

    def bit_step(it, prefix):
        cand = jnp.bitwise_or(prefix, jnp.left_shift(jnp.int32(1), 30 - it))
        cnt = jnp.sum((key >= cand).astype(I32), axis=1, keepdims=True)
        return jnp.where(cnt >= cap, cand, prefix)

    thr = lax.fori_loop(0, 31, bit_step, jnp.zeros((e, 1), I32))
    gt = key > thr
    eq = key == thr
    need = cap - jnp.sum(gt.astype(I32), axis=1, keepdims=True)
    sel = jnp.logical_or(gt, jnp.logical_and(eq, _excl_cumsum_lanes(eq.astype(I32)) < need))
    sel_i = sel.astype(I32)
    pos_scr[...] = jnp.where(sel, _excl_cumsum_lanes(sel_i), -1)

    tok = lax.broadcasted_iota(I32, (e, n), 1)
    lane = lax.broadcasted_iota(I32, off_ref.shape, 1)
    off = jnp.zeros(off_ref.shape, I32)
    for b in range(1, ntile + 1):
        cnt = jnp.sum(jnp.where(tok < b * tmtok, sel_i, 0), axis=1, keepdims=True)
        off = jnp.where(lane == b, cnt, off)
    off_ref[...] = off

    tok1 = lax.broadcasted_iota(I32, (8, n), 1)
    row = lax.broadcasted_iota(I32, (8, n), 0)
    tok_hi = jnp.right_shift(tok1, 7).astype(F32)
    tok_lo = jnp.bitwise_and(tok1, 127).astype(F32)

    def expert(ei, _):
        pos_e = pos_scr[pl.ds(ei, 1), :]
        g = aff_ref[pl.ds(ei, 1), :]
        g0 = g.astype(BF16).astype(F32)
        r1 = g - g0
        g1 = r1.astype(BF16).astype(F32)
        g2 = r1 - g1
        vals = jnp.where(row == 0, tok_hi, jnp.where(row == 1, tok_lo, jnp.where(
            row == 2, g0, jnp.where(row == 3, g1, jnp.where(row == 4, g2, 0.0))))).astype(BF16)

        def chunk(ci, _):
            slot = ci * SLOT_BLOCK + lax.broadcasted_iota(I32, (SLOT_BLOCK, n), 0)
            onehot = jnp.where(pos_e == slot, 1.0, 0.0).astype(BF16)
            r = lax.dot_general(vals, onehot, (((1,), (1,)), ((), ())), preferred_element_type=F32)
            idx_ref[ei, pl.ds(ci, 1), :] = (r[0:1] * 128.0 + r[1:2]).astype(I32)
            gate_ref[ei, pl.ds(ci, 1), :] = r[2:3] + r[3:4] + r[4:5]
            return 0

        lax.fori_loop(0, cap // SLOT_BLOCK, chunk, 0)
        return 0

    lax.fori_loop(0, e, expert, 0)


def _select(aff_t, cap, ntile, tmtok):
    e, n = aff_t.shape
    nc = cap // SLOT_BLOCK
    kern = functools.partial(_select_kernel, cap=cap, ntile=ntile, tmtok=tmtok)
    idx, gate, off = pl.pallas_call(
        kern,
        out_shape=[jax.ShapeDtypeStruct((e, nc, SLOT_BLOCK), I32),
                   jax.ShapeDtypeStruct((e, nc, SLOT_BLOCK), F32),
                   jax.ShapeDtypeStruct((e, LANES), I32)],
        scratch_shapes=[pltpu.VMEM((e, n), I32)],
        compiler_params=pltpu.CompilerParams(vmem_limit_bytes=VMEM_LIMIT),
        name="moe_select",
    )(aff_t)
    return idx.reshape(e, cap), gate.reshape(e, cap), off


def _ffn_kernel(idx_sm, *refs, ngrp, cap, nf, tf, dh):
    hp_hbm = refs[:ngrp]
    wg_ref, wu_ref, wd_ref, o_ref, xp_scr, xs_scr, hid_scr, sem = refs[ngrp:]
    e = pl.program_id(0)
    s = pl.program_id(1)
    ne = pl.num_programs(0)
    m = ngrp * cap

    def issue_gather(expert):
        for grp in range(ngrp):
            base = (grp * ne + expert) * cap

            def body(c, _):
                pltpu.make_async_copy(hp_hbm[grp].at[pl.ds(idx_sm[base + c], 1), :],
                                      xp_scr.at[pl.ds(grp * cap + c, 1), :], sem).start()
                return 0

            lax.fori_loop(0, cap, body, 0, unroll=8)

    @pl.when(jnp.logical_and(e == 0, s == 0))
    def _():
        issue_gather(0)

    @pl.when(s == 0)
    def _():
        pltpu.make_async_copy(hp_hbm[0].at[pl.ds(0, m), :], xp_scr, sem).wait()
        p = xp_scr[...]
        xs_scr[0] = lax.bitcast_convert_type(jnp.left_shift(p, 16), F32).astype(BF16)
        xs_scr[1] = lax.bitcast_convert_type(jnp.bitwise_and(p, jnp.int32(-65536)), F32).astype(BF16)

        @pl.when(e + 1 < ne)
        def _():
            issue_gather(e + 1)

    @pl.when(s < nf)
    def _():
        wg = wg_ref[0, 0].astype(BF16)
        wu = wu_ref[0, 0].astype(BF16)
        xl = xs_scr[0]
        xh = xs_scr[1]
        g = (jnp.dot(xl, wg[:dh], preferred_element_type=F32) + jnp.dot(xh, wg[dh:], preferred_element_type=F32))
        u = (jnp.dot(xl, wu[:dh], preferred_element_type=F32) + jnp.dot(xh, wu[dh:], preferred_element_type=F32))
        hid_scr[s] = (_silu(g) * u).astype(BF16)

    @pl.when(s >= nf)
    def _():
        wd = wd_ref[0, 0].astype(BF16)
        acc = jnp.dot(hid_scr[0], wd[:tf], preferred_element_type=F32)
        for k in range(1, nf):
            acc = acc + jnp.dot(hid_scr[k], wd[k * tf:(k + 1) * tf], preferred_element_type=F32)
        for grp in range(ngrp):
            o_ref[grp, 0] = acc[grp * cap:(grp + 1) * cap]


def _ffn(idx_flat, hps, w_gate, w_up, w_down, layer, cap, tf, tn):
    ngrp = len(hps)
    n, dh = hps[0].shape
    _, ne, d, f = w_gate.shape
    nf = f // tf
    nn = d // tn
    m = ngrp * cap
    assert all(hp.shape[0] >= m for hp in hps)
    kern = functools.partial(_ffn_kernel, ngrp=ngrp, cap=cap, nf=nf, tf=tf, dh=dh)
    grid_spec = pltpu.PrefetchScalarGridSpec(
        num_scalar_prefetch=1,
        grid=(ne, nf + nn),
        in_specs=[pl.BlockSpec(memory_space=pl.ANY)] * ngrp + [
            pl.BlockSpec((1, 1, d, tf), lambda e, s, idx: (layer, e, 0, jnp.minimum(s, nf - 1))),
            pl.BlockSpec((1, 1, d, tf), lambda e, s, idx: (layer, e, 0, jnp.minimum(s, nf - 1))),
            pl.BlockSpec((1, 1, f, tn), lambda e, s, idx: (layer, e, 0, jnp.maximum(s - nf, 0))),
        ],
        out_specs=pl.BlockSpec((ngrp, 1, cap, tn), lambda e, s, idx: (0, e, 0, jnp.maximum(s - nf, 0))),
        scratch_shapes=[
            pltpu.VMEM((m, dh), I32),
            pltpu.VMEM((2, m, dh), BF16),
            pltpu.VMEM((nf, m, tf), BF16),
            pltpu.SemaphoreType.DMA(()),
        ],
    )
    return pl.pallas_call(
        kern,
        grid_spec=grid_spec,
        out_shape=jax.ShapeDtypeStruct((ngrp, ne, cap, d), F32),
        compiler_params=_cparams(("arbitrary", "arbitrary")),
        name="moe_ffn",
    )(idx_flat, *hps, w_gate, w_up, w_down)


def _worklist(off, ne, ntile, nblk):
    bs = SLOT_BLOCK
    w_max = ne * (nblk + ntile - 1) + ntile
    lo = off[:, :ntile].T.reshape(-1)
    hi = off[:, 1:ntile + 1].T.reshape(-1)
    b0 = jnp.minimum(lo // bs, nblk - 1)
    nb = jnp.where(hi > lo, (hi + bs - 1) // bs - b0, 0)
    first_e = (jnp.arange(ntile * ne, dtype=I32) % ne) == 0
    nb = jnp.where(first_e, jnp.maximum(nb, 1), nb)
    end = jnp.cumsum(nb)
    total = end[-1]
    w = jnp.arange(w_max, dtype=I32)
    wc = jnp.minimum(w, total - 1)
    pair = jnp.sum((end[None, :] <= wc[:, None]).astype(I32), axis=1)
    table = jnp.stack([end - nb, b0, lo, hi], axis=1)
    onehot = pair[:, None] == jnp.arange(ntile * ne, dtype=I32)[None, :]
    got = jnp.sum(jnp.where(onehot[:, :, None], table[None], 0), axis=1)
    tile = pair // ne
    exp = pair % ne
    blk = got[:, 1] + (wc - got[:, 0])
    valid = w < total
    rlo = jnp.where(valid, jnp.clip(got[:, 2] - blk * bs, 0, bs), 0)
    rhi = jnp.where(valid, jnp.clip(got[:, 3] - blk * bs, 0, bs), 0)
    first = jnp.concatenate([jnp.ones((1,), bool), tile[1:] != tile[:-1]])
    last = jnp.concatenate([tile[1:] != tile[:-1], jnp.ones((1,), bool)])
    to_i = lambda a: a.astype(I32)
    return to_i(tile), to_i(exp), to_i(blk), to_i(rlo), to_i(rhi), to_i(first), to_i(last)


def _combine_kernel(tile_sm, exp_sm, blk_sm, rlo_sm, rhi_sm, first_sm, last_sm, idx_sm, gate_sm,
                    src_ref, x_ref, g2_ref, fg_ref, o_ref, acc_scr, *, cap, tmtok, final):
    w = pl.program_id(0)

    @pl.when(first_sm[w] == 1)
    def _():
        acc_scr[...] = jnp.zeros_like(acc_scr)

    slot0 = exp_sm[w] * cap + blk_sm[w] * SLOT_BLOCK
    tok0 = tile_sm[w] * tmtok

    def row(r, _):
        tok = idx_sm[slot0 + r] - tok0
        acc_scr[pl.ds(tok, 1), :] = acc_scr[pl.ds(tok, 1), :] + src_ref[0, 0, pl.ds(r, 1), :] * gate_sm[slot0 + r]
        return 0

    lax.fori_loop(rlo_sm[w], rhi_sm[w], row, 0)

    @pl.when(last_sm[w] == 1)
    def _():
        xn = x_ref[0] + g2_ref[0] * acc_scr[...]
        if final:
            ms = jnp.mean(xn * xn, axis=-1, keepdims=True)
            xn = xn * lax.rsqrt(ms + RMS_EPS) * fg_ref[...]
        o_ref[0] = xn


def _combine(work, idx_flat_g, gate_flat_g, outs, grp, x, mod, final_g, tmtok, final):
    b, l, d = x.shape
    ne, cap = outs.shape[1], outs.shape[2]
    tpb = l // tmtok
    w_max = work[0].shape[0]
    kern = functools.partial(_combine_kernel, cap=cap, tmtok=tmtok, final=final)
    nj = 1
    grid_spec = pltpu.PrefetchScalarGridSpec(
        num_scalar_prefetch=9,
        grid=(w_max,),
        in_specs=[
            pl.BlockSpec((1, 1, SLOT_BLOCK, d),
                         lambda w, ti, ex, bl, *_: (grp, ex[w], bl[w], 0)),
            pl.BlockSpec((1, tmtok, d), lambda w, ti, *_: (ti[w] // tpb, ti[w] % tpb, 0),
                         pipeline_mode=pl.Buffered(1)),
            pl.BlockSpec((1, 1, d), lambda w, ti, *_: (ti[w] // tpb, 0, 5 * nj)),
            pl.BlockSpec((1, d), lambda w, *_: (0, 0)),
        ],
        out_specs=pl.BlockSpec((1, tmtok, d), lambda w, ti, *_: (ti[w] // tpb, ti[w] % tpb, 0)),
        scratch_shapes=[pltpu.VMEM((tmtok, d), F32)],
    )
    return pl.pallas_call(
        kern,
        grid_spec=grid_spec,
        out_shape=jax.ShapeDtypeStruct((b, l, d), F32),
        compiler_params=_cparams(("arbitrary",)),
        name="moe_combine",
    )(*work, idx_flat_g, gate_flat_g, outs, x, mod, final_g.reshape(1, d))


def _pick(n, pref):
    t = min(pref, n)
    while n % t:
        t //= 2
    return t


def _moe_layer(xs, mods, norm_g, w_router, w_gate, w_up, w_down, layer, final_g, final):
    ne = w_router.shape[1]
    hps, idxs, gates, works = [], [], [], []
    caps = set()
    for x, mod in zip(xs, mods):
        b, l, d = x.shape
        n = b * l
        cap = EC_CAPACITY_FACTOR * n // ne
        caps.add(cap)
        tmtok = _pick(l, 1024)
        ntile = n // tmtok
        hp, aff_t = _router(x, norm_g, mod, w_router, _pick(l, 512))
        idx, gate, off = _select(aff_t, cap, ntile, tmtok)
        hps.append(hp.reshape(n, d // 2))
        idxs.append(idx.reshape(-1))
        gates.append(gate.reshape(-1))
        works.append(_worklist(off, ne, ntile, cap // SLOT_BLOCK))
    assert len(caps) == 1, "request groups must have equal expert capacity"
    cap = caps.pop()
    f = w_gate.shape[3]
    outs = _ffn(jnp.concatenate(idxs), hps, w_gate, w_up, w_down, layer, cap,
                _pick(f, 256), _pick(xs[0].shape[2], 256))
    new = []
    for grp, (x, mod) in enumerate(zip(xs, mods)):
        tmtok = _pick(x.shape[1], 1024)
        new.append(_combine(works[grp], idxs[grp], gates[grp], outs, grp, x, mod, final_g, tmtok, final))
    return new


def kernel(x_prompt, x_sample, c_prompt, c_sample, ada_w, ada_b, norm_mix_g, norm_ffn_g, final_norm_g, conv_w_in, conv_b_in, conv_w_dw, conv_b_dw, conv_ln_g, conv_ln_b, conv_w_out, conv_b_out, ssm_lambda_re, ssm_lambda_im, ssm_log_step, ssm_b_re, ssm_b_im, ssm_c_re, ssm_c_im, ssm_d, ssm_w_glu, ssm_b_glu, moe_w_router, moe_w_gate, moe_w_up, moe_w_down):
    depth = ada_w.shape[0]
    d = x_prompt.shape[-1]
    bp, bs = x_prompt.shape[0], x_sample.shape[0]
    nrow = 8 * ((bp + bs + 7) // 8)
    c_all = jnp.concatenate([c_prompt, c_sample, jnp.zeros((nrow - bp - bs, d), F32)], axis=0)
    mod_all = _ada(c_all, ada_w, ada_b)

    xs = [x_prompt, x_sample]
    for i in range(depth):
        mods = [mod_all[i, :bp].reshape(bp, 1, 6 * d), mod_all[i, bp:bp + bs].reshape(bs, 1, 6 * d)]
        j = i // 2
        if i % 2 == 0:
            w_in = conv_w_in[j].astype(BF16)
            w_out = conv_w_out[j].astype(BF16)
            nxt = []
            for x, mod in zip(xs, mods):
                l = x.shape[1]
                u = _conv_in(x, norm_mix_g[i], mod, w_in, conv_b_in[j], _pick(l, 512), _pick(d, 512))
                nxt.append(_conv_out(u, x, mod, conv_w_dw[j], conv_b_dw[j], conv_ln_g[j], conv_ln_b[j],
                                     w_out, conv_b_out[j], _pick(l, 512), _pick(d, 512)))
            xs = nxt
        else:
            lay = _s5_layout(ssm_lambda_re[j], ssm_lambda_im[j], ssm_log_step[j], ssm_b_re[j], ssm_b_im[j],
                             ssm_c_re[j], ssm_c_im[j])
            w_glu = ssm_w_glu[j].astype(BF16)
            nxt = []
            for x, mod in zip(xs, mods):
                l = x.shape[1]
                y2 = _s5_scan(x, norm_mix_g[i], mod, lay, _pick(l, 256))
                nxt.append(_s5_glu(y2, x, norm_mix_g[i], mod, ssm_d[j], w_glu, ssm_b_glu[j],
                                   _pick(l, 512), _pick(d, 512)))
            xs = nxt
        xs = _moe_layer(xs, mods, norm_ffn_g[i], moe_w_router[i], moe_w_gate, moe_w_up, moe_w_down, i,
                        final_norm_g, final=(i == depth - 1))
    return (xs[0], xs[1])
```

```python
import functools
import math

import jax
import jax.numpy as jnp
from jax import lax
from jax.experimental import pallas as pl
from jax.experimental.pallas import tpu as pltpu

F32 = jnp.float32
BF16 = jnp.bfloat16
I32 = jnp.int32
HIGHEST = lax.Precision.HIGHEST

RMS_EPS = 1e-6
LN_EPS = 1e-5
CONV_WIDTH = 31
CONV_PAD = (CONV_WIDTH - 1) // 2
CONV_HALO = 16
SSM_GROUP = 16
SSM_STATE = 64
SSM_BLOCKS = 8
S5_LAG = 4
N_EXPERTS = 16
EC_CAPACITY_FACTOR = 2
LANES = 128
SLOT_BLOCK = 128
COMBINE_ROWS = 4
ROW_CHUNK = 16

VMEM_LIMIT = 56 * 1024 * 1024


def _cparams(sem, vmem=VMEM_LIMIT):
    return pltpu.CompilerParams(dimension_semantics=sem, vmem_limit_bytes=vmem)


def _modnorm_rows(x_ref, g_ref, sh_ref, sc_ref, inv_scr, emit):
    x = x_ref[0]
    inv_scr[...] = lax.rsqrt(jnp.mean(x * x, axis=-1, keepdims=True) + RMS_EPS)
    gain = g_ref[...] * (1.0 + sc_ref[0])
    shift = sh_ref[0]

    def chunk(r, _):
        rows = pl.ds(pl.multiple_of(r * ROW_CHUNK, ROW_CHUNK), ROW_CHUNK)
        emit(rows, x_ref[0, rows, :] * inv_scr[rows, :] * gain + shift)
        return 0

    lax.fori_loop(0, x.shape[0] // ROW_CHUNK, chunk, 0, unroll=2)


def _silu(x):
    return x * jax.nn.sigmoid(x)


def _gelu_tanh(x):
    c = math.sqrt(2.0 / math.pi)
    return 0.5 * x * (1.0 + jnp.tanh(c * (x + 0.044715 * (x * x * x))))


def _ada_kernel(c_ref, w_ref, b_ref, o_ref):
    c = c_ref[...]
    cond = _silu(c)
    o_ref[0] = jnp.dot(cond, w_ref[0], precision=HIGHEST, preferred_element_type=F32) + b_ref[0]


def _ada(c_all, ada_w, ada_b):
    depth, d, n6 = ada_w.shape
    rows = c_all.shape[0]
    tn = _pick(n6, 1024)
    return pl.pallas_call(
        _ada_kernel,
        grid=(depth, n6 // tn),
        in_specs=[
            pl.BlockSpec((rows, d), lambda i, j: (0, 0)),
            pl.BlockSpec((1, d, tn), lambda i, j: (i, 0, j)),
            pl.BlockSpec((1, 1, tn), lambda i, j: (i, 0, j)),
        ],
        out_specs=pl.BlockSpec((1, rows, tn), lambda i, j: (i, 0, j)),
        out_shape=jax.ShapeDtypeStruct((depth, rows, n6), F32),
        compiler_params=_cparams(("parallel", "parallel")),
        name="ada_mod",
    )(c_all, ada_w, ada_b.reshape(depth, 1, n6))


def _conv_in_kernel(x_ref, g_ref, sh_ref, sc_ref, wa_ref, wg_ref, ba_ref, bg_ref, o_ref, h_scr, inv_scr):
    @pl.when(pl.program_id(2) == 0)
    def _():
        def emit(rows, h):
            h_scr[rows, :] = h.astype(BF16)

        _modnorm_rows(x_ref, g_ref, sh_ref, sc_ref, inv_scr, emit)

    h = h_scr[...]
    a = jnp.dot(h, wa_ref[...], preferred_element_type=F32) + ba_ref[...]
    g = jnp.dot(h, wg_ref[...], preferred_element_type=F32) + bg_ref[...]
    o_ref[0] = a * jax.nn.sigmoid(g)


def _conv_in(x, norm_g, mod, w_in_bf, b_in, tm, tn):
    b, l, d = x.shape
    nj = d // tn
    return pl.pallas_call(
        _conv_in_kernel,
        grid=(b, l // tm, nj),
        in_specs=[
            pl.BlockSpec((1, tm, d), lambda bi, i, j: (bi, i, 0)),
            pl.BlockSpec((1, d), lambda bi, i, j: (0, 0)),
            pl.BlockSpec((1, 1, d), lambda bi, i, j: (bi, 0, 0)),
            pl.BlockSpec((1, 1, d), lambda bi, i, j: (bi, 0, 1)),
            pl.BlockSpec((d, tn), lambda bi, i, j: (0, j)),
            pl.BlockSpec((d, tn), lambda bi, i, j: (0, j + nj)),
            pl.BlockSpec((1, tn), lambda bi, i, j: (0, j)),
            pl.BlockSpec((1, tn), lambda bi, i, j: (0, j + nj)),
        ],
        out_specs=pl.BlockSpec((1, tm, tn), lambda bi, i, j: (bi, i, j)),
        out_shape=jax.ShapeDtypeStruct((b, l, d), F32),
        scratch_shapes=[pltpu.VMEM((tm, d), BF16), pltpu.VMEM((tm, 1), F32)],
        compiler_params=_cparams(("parallel", "parallel", "arbitrary")),
        name="conv_in_glu",
    )(x, norm_g.reshape(1, d), mod, mod, w_in_bf, w_in_bf, b_in.reshape(1, 2 * d), b_in.reshape(1, 2 * d))


def _conv_out_kernel(u_ref, up_ref, un_ref, wdw_ref, bdw_ref, lg_ref, lb_ref, wo_ref, bo_ref, x_ref, g1_ref,
                     o_ref, ext_scr, c_scr, v_scr, *, tm, d, rc, cc):
    i = pl.program_id(1)
    ni = pl.num_programs(1)

    ncol = d // cc

    @pl.when(pl.program_id(2) == 0)
    def _():
        for c in range(ncol):
            cols = pl.ds(c * cc, cc)
            ext_scr[c, pl.ds(0, CONV_HALO), :] = jnp.where(i > 0, up_ref[0, :, cols], 0.0)
            ext_scr[c, pl.ds(CONV_HALO, tm), :] = u_ref[0, :, cols]
            ext_scr[c, pl.ds(CONV_HALO + tm, CONV_HALO), :] = jnp.where(i < ni - 1, un_ref[0, :, cols], 0.0)

        def col_chunk(c, _):
            for r0 in range(0, tm, rc):
                acc = jnp.zeros((rc, cc), F32) + bdw_ref[c]
                for k in range(CONV_WIDTH):
                    seg = ext_scr.at[c][pl.ds(r0 + (CONV_HALO - CONV_PAD + k), rc, stride=1), :]
                    acc = acc + seg * wdw_ref[c, pl.ds(k, 1), :]
                c_scr[c, pl.ds(r0, rc), :] = acc
            return 0

        lax.fori_loop(0, ncol, col_chunk, 0)

        def ln_chunk(r, _):
            r0 = pl.multiple_of(r * rc, rc)
            xc = [c_scr[c, pl.ds(r0, rc), :] for c in range(ncol)]
            mu = sum(jnp.sum(v, axis=-1, keepdims=True) for v in xc) * (1.0 / d)
            xm = [v - mu for v in xc]
            var = sum(jnp.sum(v * v, axis=-1, keepdims=True) for v in xm) * (1.0 / d)
            inv = lax.rsqrt(var + LN_EPS)
            for c in range(ncol):
                cols = pl.ds(c * cc, cc)
                y = xm[c] * inv * lg_ref[:, cols] + lb_ref[:, cols]
                v_scr[pl.ds(r0, rc), cols] = _silu(y).astype(BF16)
            return 0

        lax.fori_loop(0, tm // rc, ln_chunk, 0)

    o = jnp.dot(v_scr[...], wo_ref[...], preferred_element_type=F32) + bo_ref[...]
    o_ref[0] = x_ref[0] + g1_ref[0] * o


def _conv_out(u, x, mod, w_dw, b_dw, ln_g, ln_b, w_out_bf, b_out, tm, tn):
    b, l, d = x.shape
    nj = d // tn
    hb = tm // CONV_HALO
    nhb = l // CONV_HALO
    rc = min(64, tm)
    cc = min(LANES, d)
    ncol = d // cc
    w_dw_p = jnp.concatenate([w_dw, jnp.zeros((1, d), w_dw.dtype)], axis=0)
    w_dw_p = w_dw_p.reshape(CONV_WIDTH + 1, ncol, cc).transpose(1, 0, 2)
    kern = functools.partial(_conv_out_kernel, tm=tm, d=d, rc=rc, cc=cc)
    return pl.pallas_call(
        kern,
        grid=(b, l // tm, nj),
        in_specs=[
            pl.BlockSpec((1, tm, d), lambda bi, i, j: (bi, i, 0)),
            pl.BlockSpec((1, CONV_HALO, d), lambda bi, i, j: (bi, jnp.maximum(i * hb - 1, 0), 0)),
            pl.BlockSpec((1, CONV_HALO, d), lambda bi, i, j: (bi, jnp.minimum((i + 1) * hb, nhb - 1), 0)),
            pl.BlockSpec((ncol, CONV_WIDTH + 1, cc), lambda bi, i, j: (0, 0, 0)),
            pl.BlockSpec((ncol, 1, cc), lambda bi, i, j: (0, 0, 0)),
            pl.BlockSpec((1, d), lambda bi, i, j: (0, 0)),
            pl.BlockSpec((1, d), lambda bi, i, j: (0, 0)),
            pl.BlockSpec((d, tn), lambda bi, i, j: (0, j)),
            pl.BlockSpec((1, tn), lambda bi, i, j: (0, j)),
            pl.BlockSpec((1, tm, tn), lambda bi, i, j: (bi, i, j)),
            pl.BlockSpec((1, 1, tn), lambda bi, i, j: (bi, 0, 2 * nj + j)),
        ],
        out_specs=pl.BlockSpec((1, tm, tn), lambda bi, i, j: (bi, i, j)),
        out_shape=jax.ShapeDtypeStruct((b, l, d), F32),
        scratch_shapes=[
            pltpu.VMEM((ncol, tm + 2 * CONV_HALO, cc), F32),
            pltpu.VMEM((ncol, tm, cc), F32),
            pltpu.VMEM((tm, d), BF16),
        ],
        compiler_params=_cparams(("parallel", "parallel", "arbitrary")),
        name="conv_dw_ln_out",
    )(u, u, u, w_dw_p, b_dw.reshape(ncol, 1, cc), ln_g.reshape(1, d), ln_b.reshape(1, d), w_out_bf,
      b_out.reshape(1, d), x, mod)


def _zoh_discretise(lam_re, lam_im, log_step, b_re, b_im):
    dt = jnp.exp(log_step)[..., None]
    mag = jnp.exp(lam_re * dt)
    a_re = mag * jnp.cos(lam_im * dt)
    a_im = mag * jnp.sin(lam_im * dt)
    nr = a_re - 1.0
    ni = a_im
    den = lam_re * lam_re + lam_im * lam_im
    k_re = (nr * lam_re + ni * lam_im) / den
    k_im = (ni * lam_re - nr * lam_im) / den
    bb_re = k_re[..., None] * b_re - k_im[..., None] * b_im
    bb_im = k_re[..., None] * b_im + k_im[..., None] * b_re
    return a_re, a_im, bb_re, bb_im


def _s5_layout(lam_re, lam_im, log_step, b_re, b_im, c_re, c_im):
    a_re, a_im, bb_re, bb_im = _zoh_discretise(lam_re, lam_im, log_step, b_re, b_im)
    nd, g, p = a_re.shape
    c = b_re.shape[-1]
    gpb = g // SSM_BLOCKS
    eye = jnp.eye(gpb, dtype=F32)

    def expand(bb):
        bb = bb.reshape(nd, SSM_BLOCKS, gpb, p, c)
        return jnp.einsum('dsgpc,gh->dsgchp', bb, eye).reshape(nd, SSM_BLOCKS, gpb * c, gpb * p).astype(BF16)

    def contract(cc):
        cc = cc.reshape(nd, SSM_BLOCKS, gpb, c, p)
        return jnp.einsum('dsgcp,gh->dshpgc', cc, eye).reshape(nd, SSM_BLOCKS, gpb * p, gpb * c).astype(BF16)

    return (a_re.reshape(nd, SSM_BLOCKS, gpb * p), a_im.reshape(nd, SSM_BLOCKS, gpb * p),
            expand(bb_re), expand(bb_im), contract(c_re), contract(c_im))


def _s5_kernel(x_ref, g_ref, sh_ref, sc_ref, ar_ref, ai_ref, wer_ref, wei_ref, wcr_ref, wci_ref,
               o_ref, sre, sim, cre, cim, h_scr, inv_scr, *, t, chb, nsb, pitch):
    dirn = pl.program_id(0)
    nslab = nsb // LANES

    @pl.when(jnp.logical_and(jnp.logical_and(dirn == 0, pl.program_id(1) == 0), pl.program_id(2) == 0))
    def _():
        sre[...] = jnp.zeros_like(sre)
        sim[...] = jnp.zeros_like(sim)

    @pl.when(pl.program_id(2) == 0)
    def _():
        cre[...] = jnp.zeros_like(cre)
        cim[...] = jnp.zeros_like(cim)

    def emit(rows, h):
        h_scr[rows, :] = h.astype(BF16)

    _modnorm_rows(x_ref, g_ref, sh_ref, sc_ref, inv_scr, emit)

    def base(s):
        return s * pitch + (S5_LAG if s % 2 else 0)

    for s in range(SSM_BLOCKS):
        hs = h_scr[:, s * chb:(s + 1) * chb]
        bur = jnp.dot(hs, wer_ref[0, s], preferred_element_type=F32)
        bui = jnp.dot(hs, wei_ref[0, s], preferred_element_type=F32)
        for j in range(nslab):
            sre[j, pl.ds(base(s), t), :] = bur[:, j * LANES:(j + 1) * LANES]
            sim[j, pl.ds(base(s), t), :] = bui[:, j * LANES:(j + 1) * LANES]

    ar = [ar_ref[0, :, j * LANES:(j + 1) * LANES] for j in range(nslab)]
    ai = [ai_ref[0, :, j * LANES:(j + 1) * LANES] for j in range(nslab)]
    odd = jnp.bitwise_and(lax.broadcasted_iota(I32, (SSM_BLOCKS, LANES), 0), 1)
    nrow = t + S5_LAG

    def make_step(active):
        def step(k, carry):
            xr, xi = carry
            row = jnp.where(dirn == 0, k, nrow - 1 - k)
            rows = pl.ds(row, SSM_BLOCKS, stride=pitch)
            nr, ni = [], []
            for j in range(nslab):
                br = sre.at[j][rows, :]
                bi = sim.at[j][rows, :]
                r = ar[j] * xr[j] - ai[j] * xi[j] + br
                im = ar[j] * xi[j] + ai[j] * xr[j] + bi
                sre.at[j][rows, :] = r
                sim.at[j][rows, :] = im
                if active is not None:
                    r = jnp.where(active, r, xr[j])
                    im = jnp.where(active, im, xi[j])
                nr.append(r)
                ni.append(im)
            return tuple(nr), tuple(ni)
        return step

    x0 = (tuple(cre[:, j * LANES:(j + 1) * LANES] for j in range(nslab)),
          tuple(cim[:, j * LANES:(j + 1) * LANES] for j in range(nslab)))
    x1 = lax.fori_loop(0, S5_LAG, make_step(odd == dirn), x0)
    x2 = lax.fori_loop(S5_LAG, t, make_step(None), x1, unroll=2)
    xr, xi = lax.fori_loop(t, nrow, make_step(odd != dirn), x2)
    for j in range(nslab):
        cre[:, j * LANES:(j + 1) * LANES] = xr[j]
        cim[:, j * LANES:(j + 1) * LANES] = xi[j]

    for s in range(SSM_BLOCKS):
        sr = jnp.concatenate([sre[j, pl.ds(base(s), t), :] for j in range(nslab)], axis=1).astype(BF16)
        si = jnp.concatenate([sim[j, pl.ds(base(s), t), :] for j in range(nslab)], axis=1).astype(BF16)
        y = (jnp.dot(sr, wcr_ref[0, s], preferred_element_type=F32)
             - jnp.dot(si, wci_ref[0, s], preferred_element_type=F32))
        o_ref[0, 0, :, s * chb:(s + 1) * chb] = y


def _s5_scan(x, norm_g, mod, lay, t):
    a_re, a_im, we_re, we_im, wc_re, wc_im = lay
    b, l, d = x.shape
    nd = a_re.shape[0]
    chb = d // SSM_BLOCKS
    nsb = a_re.shape[-1]
    nslab = nsb // LANES
    nt = l // t
    pitch = t + S5_LAG
    kern = functools.partial(_s5_kernel, t=t, chb=chb, nsb=nsb, pitch=pitch)

    def tile(dd, i):
        return i + dd * (nt - 1 - 2 * i)

    single = pl.Buffered(1)
    return pl.pallas_call(
        kern,
        grid=(nd, b, nt),
        in_specs=[
            pl.BlockSpec((1, t, d), lambda dd, bi, i: (bi, tile(dd, i), 0)),
            pl.BlockSpec((1, d), lambda dd, bi, i: (0, 0)),
            pl.BlockSpec((1, 1, d), lambda dd, bi, i: (bi, 0, 0)),
            pl.BlockSpec((1, 1, d), lambda dd, bi, i: (bi, 0, 1)),
            pl.BlockSpec((1, SSM_BLOCKS, nsb), lambda dd, bi, i: (dd, 0, 0)),
            pl.BlockSpec((1, SSM_BLOCKS, nsb), lambda dd, bi, i: (dd, 0, 0)),
            pl.BlockSpec((1, SSM_BLOCKS, chb, nsb), lambda dd, bi, i: (dd, 0, 0, 0), pipeline_mode=single),
            pl.BlockSpec((1, SSM_BLOCKS, chb, nsb), lambda dd, bi, i: (dd, 0, 0, 0), pipeline_mode=single),
            pl.BlockSpec((1, SSM_BLOCKS, nsb, chb), lambda dd, bi, i: (dd, 0, 0, 0), pipeline_mode=single),
            pl.BlockSpec((1, SSM_BLOCKS, nsb, chb), lambda dd, bi, i: (dd, 0, 0, 0), pipeline_mode=single),
        ],
        out_specs=pl.BlockSpec((1, 1, t, d), lambda dd, bi, i: (dd, bi, tile(dd, i), 0)),
        out_shape=jax.ShapeDtypeStruct((nd, b, l, d), F32),
        scratch_shapes=[
            pltpu.VMEM((nslab, SSM_BLOCKS * pitch, LANES), F32),
            pltpu.VMEM((nslab, SSM_BLOCKS * pitch, LANES), F32),
            pltpu.VMEM((SSM_BLOCKS, nsb), F32),
            pltpu.VMEM((SSM_BLOCKS, nsb), F32),
            pltpu.VMEM((t, d), BF16),
            pltpu.VMEM((t, 1), F32),
        ],
        compiler_params=_cparams(("arbitrary", "arbitrary", "arbitrary")),
        name="s5_scan",
    )(x, norm_g.reshape(1, d), mod, mod, a_re, a_im, we_re, we_im, wc_re, wc_im)


def _s5_glu_kernel(yf_ref, yb_ref, xf_ref, g_ref, sh_ref, sc_ref, dsk_ref, wa_ref, wg_ref, ba_ref, bg_ref,
                   x_ref, g1_ref, o_ref, v_scr, inv_scr):
    @pl.when(pl.program_id(2) == 0)
    def _():
        def emit(rows, h):
            y = yf_ref[0, 0, rows, :] + yb_ref[0, 0, rows, :] + dsk_ref[...] * h
            v_scr[rows, :] = _gelu_tanh(y).astype(BF16)

        _modnorm_rows(xf_ref, g_ref, sh_ref, sc_ref, inv_scr, emit)

    v = v_scr[...]
    a = jnp.dot(v, wa_ref[...], preferred_element_type=F32) + ba_ref[...]
    g = jnp.dot(v, wg_ref[...], preferred_element_type=F32) + bg_ref[...]
    o_ref[0] = x_ref[0] + g1_ref[0] * (a * jax.nn.sigmoid(g))


def _s5_glu(y2, x, norm_g, mod, d_skip, w_glu_bf, b_glu, tm, tn):
    b, l, d = x.shape
    nj = d // tn
    return pl.pallas_call(
        _s5_glu_kernel,
        grid=(b, l // tm, nj),
        in_specs=[
            pl.BlockSpec((1, 1, tm, d), lambda bi, i, j: (0, bi, i, 0)),
            pl.BlockSpec((1, 1, tm, d), lambda bi, i, j: (1, bi, i, 0)),
            pl.BlockSpec((1, tm, d), lambda bi, i, j: (bi, i, 0)),
            pl.BlockSpec((1, d), lambda bi, i, j: (0, 0)),
            pl.BlockSpec((1, 1, d), lambda bi, i, j: (bi, 0, 0)),
            pl.BlockSpec((1, 1, d), lambda bi, i, j: (bi, 0, 1)),
            pl.BlockSpec((1, d), lambda bi, i, j: (0, 0)),
            pl.BlockSpec((d, tn), lambda bi, i, j: (0, j)),
            pl.BlockSpec((d, tn), lambda bi, i, j: (0, j + nj)),
            pl.BlockSpec((1, tn), lambda bi, i, j: (0, j)),
            pl.BlockSpec((1, tn), lambda bi, i, j: (0, j + nj)),
            pl.BlockSpec((1, tm, tn), lambda bi, i, j: (bi, i, j)),
            pl.BlockSpec((1, 1, tn), lambda bi, i, j: (bi, 0, 2 * nj + j)),
        ],
        out_specs=pl.BlockSpec((1, tm, tn), lambda bi, i, j: (bi, i, j)),
        out_shape=jax.ShapeDtypeStruct((b, l, d), F32),
        scratch_shapes=[pltpu.VMEM((tm, d), BF16), pltpu.VMEM((tm, 1), F32)],
        compiler_params=_cparams(("parallel", "parallel", "arbitrary")),
        name="s5_gelu_glu",
    )(y2, y2, x, norm_g.reshape(1, d), mod, mod, d_skip.reshape(1, d), w_glu_bf, w_glu_bf,
      b_glu.reshape(1, 2 * d), b_glu.reshape(1, 2 * d), x, mod)


def _router_kernel(x_ref, g_ref, sh_ref, sc_ref, wr_ref, hp_ref, aff_ref, hhi_scr, hlo_scr, inv_scr, *, dh):
    def emit(rows, h):
        hb = h.astype(BF16)
        hbf = hb.astype(F32)
        hhi_scr[rows, :] = hb
        hlo_scr[rows, :] = (h - hbf).astype(BF16)
        bits = lax.bitcast_convert_type(hbf, I32)
        lo = lax.shift_right_logical(bits[:, :dh], 16)
        hi = jnp.bitwise_and(bits[:, dh:], jnp.int32(-65536))
        hp_ref[0, rows, :] = jnp.bitwise_or(hi, lo)

    _modnorm_rows(x_ref, g_ref, sh_ref, sc_ref, inv_scr, emit)

    w = wr_ref[...]
    w_hi = w.astype(BF16)
    w_lo = (w - w_hi.astype(F32)).astype(BF16)
    nt = (((1,), (1,)), ((), ()))
    h_hi = hhi_scr[...]
    logits = (lax.dot_general(w_hi, h_hi, nt, preferred_element_type=F32)
              + lax.dot_general(w_lo, h_hi, nt, preferred_element_type=F32)
              + lax.dot_general(w_hi, hlo_scr[...], nt, preferred_element_type=F32))
    m = jnp.max(logits, axis=0, keepdims=True)
    ex = jnp.exp(logits - m)
    aff_ref[...] = ex / jnp.sum(ex, axis=0, keepdims=True)


def _router(x, norm_g, mod, w_router, tm):
    b, l, d = x.shape
    e = w_router.shape[1]
    dh = d // 2
    nt = l // tm
    kern = functools.partial(_router_kernel, dh=dh)
    return pl.pallas_call(
        kern,
        grid=(b, nt),
        in_specs=[
            pl.BlockSpec((1, tm, d), lambda bi, i: (bi, i, 0)),
            pl.BlockSpec((1, d), lambda bi, i: (0, 0)),
            pl.BlockSpec((1, 1, d), lambda bi, i: (bi, 0, 3)),
            pl.BlockSpec((1, 1, d), lambda bi, i: (bi, 0, 4)),
            pl.BlockSpec((e, d), lambda bi, i: (0, 0)),
        ],
        out_specs=[
            pl.BlockSpec((1, tm, dh), lambda bi, i: (bi, i, 0)),
            pl.BlockSpec((e, tm), lambda bi, i: (0, bi * nt + i)),
        ],
        out_shape=[jax.ShapeDtypeStruct((b, l, dh), I32), jax.ShapeDtypeStruct((e, b * l), F32)],
        scratch_shapes=[pltpu.VMEM((tm, d), BF16), pltpu.VMEM((tm, d), BF16), pltpu.VMEM((tm, 1), F32)],
        compiler_params=_cparams(("parallel", "parallel")),
        name="moe_router",
    )(x, norm_g.reshape(1, d), mod, mod, w_router.T)


def _excl_cumsum_lanes(x):
    n = x.shape[1]
    lane = lax.broadcasted_iota(I32, x.shape, 1)
    inc = x
    k = 1
    while k < n:
        inc = inc + jnp.where(lane >= k, pltpu.roll(inc, k, axis=1), 0)
        k *= 2
    return inc - x


def _select_kernel(aff_ref, idx_ref, gate_ref, off_ref, pos_scr, vals_scr, racc, off_sm, sem, *, cap, ntile, tmtok):
    e, n = aff_ref.shape
    key = lax.bitcast_convert_type(aff_ref[...], I32)

    def bit_step(it, prefix):
        cand = jnp.bitwise_or(prefix, jnp.left_shift(jnp.int32(1), 30 - it))
        cnt = jnp.sum((key >= cand).astype(I32), axis=1, keepdims=True)
        return jnp.where(cnt >= cap, cand, prefix)

    thr = lax.fori_loop(0, 31, bit_step, jnp.zeros((e, 1), I32))
    gt = key > thr
    eq = key == thr
    need = cap - jnp.sum(gt.astype(I32), axis=1, keepdims=True)
    sel = jnp.logical_or(gt, jnp.logical_and(eq, _excl_cumsum_lanes(eq.astype(I32)) < need))
    sel_i = sel.astype(I32)
    pos_scr[...] = jnp.where(sel, _excl_cumsum_lanes(sel_i), -1)

    tok = lax.broadcasted_iota(I32, (e, n), 1)
    lane = lax.broadcasted_iota(I32, off_ref.shape, 1)
    off = jnp.zeros(off_ref.shape, I32)
    for b in range(1, ntile + 1):
        cnt = jnp.sum(jnp.where(tok < b * tmtok, sel_i, 0), axis=1, keepdims=True)
        off = jnp.where(lane == b, cnt, off)
    off_ref[...] = off
    off_copy = pltpu.make_async_copy(off_ref, off_sm, sem)
    off_copy.start()

    tok1 = lax.broadcasted_iota(I32, vals_scr.shape, 1)
    row = lax.broadcasted_iota(I32, vals_scr.shape, 0)
    tok_hi = jnp.right_shift(tok1, 7).astype(F32)
    tok_lo = jnp.bitwise_and(tok1, 127).astype(F32)
    off_copy.wait()

    def expert(ei, _):
        g = aff_ref[pl.ds(ei, 1), :]
        g0 = g.astype(BF16).astype(F32)
        r1 = g - g0
        g1 = r1.astype(BF16).astype(F32)
        g2 = r1 - g1
        vals_scr[...] = jnp.where(row == 0, tok_hi, jnp.where(row == 1, tok_lo, jnp.where(
            row == 2, g0, jnp.where(row == 3, g1, jnp.where(row == 4, g2, 0.0))))).astype(BF16)

        def chunk(ci, _):
            s0 = ci * SLOT_BLOCK
            slot = s0 + lax.broadcasted_iota(I32, (SLOT_BLOCK, tmtok), 0)
            racc[...] = jnp.zeros_like(racc)
            for tb in range(ntile):
                @pl.when(jnp.logical_and(off_sm[ei, tb] < s0 + SLOT_BLOCK, off_sm[ei, tb + 1] > s0))
                def _():
                    cols = pl.ds(tb * tmtok, tmtok)
                    onehot = jnp.where(pos_scr[pl.ds(ei, 1), cols] == slot, 1.0, 0.0).astype(BF16)
                    racc[...] += lax.dot_general(vals_scr[:, cols], onehot, (((1,), (1,)), ((), ())),
                                                 preferred_element_type=F32)
            r = racc[...]
            idx_ref[ei, pl.ds(ci, 1), :] = (r[0:1] * 128.0 + r[1:2]).astype(I32)
            gate_ref[ei, pl.ds(ci, 1), :] = r[2:3] + r[3:4] + r[4:5]
            return 0

        lax.fori_loop(0, cap // SLOT_BLOCK, chunk, 0)
        return 0

    lax.fori_loop(0, e, expert, 0)


def _select(aff_t, cap, ntile, tmtok):
    e, n = aff_t.shape
    nc = cap // SLOT_BLOCK
    kern = functools.partial(_select_kernel, cap=cap, ntile=ntile, tmtok=tmtok)
    idx, gate, off = pl.pallas_call(
        kern,
        out_shape=[jax.ShapeDtypeStruct((e, nc, SLOT_BLOCK), I32),
                   jax.ShapeDtypeStruct((e, nc, SLOT_BLOCK), F32),
                   jax.ShapeDtypeStruct((e, LANES), I32)],
        scratch_shapes=[pltpu.VMEM((e, n), I32), pltpu.VMEM((16, n), BF16), pltpu.VMEM((16, SLOT_BLOCK), F32),
                        pltpu.SMEM((e, LANES), I32), pltpu.SemaphoreType.DMA(())],
        compiler_params=pltpu.CompilerParams(vmem_limit_bytes=VMEM_LIMIT),
        name="moe_select",
    )(aff_t)
    return idx.reshape(e, cap), gate.reshape(e, cap), off


def _ffn_kernel(idx_sm, *refs, ngrp, cap, nf, tf, dh, rows_per_step):
    hp_hbm = refs[:ngrp]
    wg_ref, wu_ref, wd_ref, o_ref, xp_scr, xs_scr, hid_scr, sem = refs[ngrp:]
    e = pl.program_id(0)
    s = pl.program_id(1)
    ne = pl.num_programs(0)
    m = ngrp * cap

    def issue_gather(expert):
        for grp in range(ngrp):
            base = (grp * ne + expert) * cap

            def body(c, _):
                pltpu.make_async_copy(hp_hbm[grp].at[pl.ds(idx_sm[base + c], 1), :],
                                      xp_scr.at[pl.ds(grp * cap + c, 1), :], sem).start()
                return 0

            lax.fori_loop(0, cap, body, 0, unroll=8)

    def wait_gather():
        pltpu.make_async_copy(hp_hbm[0].at[pl.ds(0, m), :], xp_scr, sem).wait()

    nstep = pl.num_programs(1)

    @pl.when(jnp.logical_and(e == 0, s == 0))
    def _():
        issue_gather(0)

    @pl.when(s == 0)
    def _():
        wait_gather()
        p = xp_scr[...]
        xs_scr[0] = lax.bitcast_convert_type(jnp.left_shift(p, 16), F32).astype(BF16)
        xs_scr[1] = lax.bitcast_convert_type(jnp.bitwise_and(p, jnp.int32(-65536)), F32).astype(BF16)

    nxt = jnp.minimum(e + 1, ne - 1)

    def issue_slice():
        for grp in range(ngrp):
            base = (grp * ne + nxt) * cap + s * rows_per_step
            for c in range(rows_per_step):
                pltpu.make_async_copy(hp_hbm[grp].at[pl.ds(idx_sm[base + c], 1), :],
                                      xp_scr.at[pl.ds(grp * cap + s * rows_per_step + c, 1), :], sem).start()

    @pl.when(s < nf)
    def _():
        issue_slice()
        wg = wg_ref[0, 0].astype(BF16)
        wu = wu_ref[0, 0].astype(BF16)
        xl = xs_scr[0]
        xh = xs_scr[1]
        g = (jnp.dot(xl, wg[:dh], preferred_element_type=F32) + jnp.dot(xh, wg[dh:], preferred_element_type=F32))
        u = (jnp.dot(xl, wu[:dh], preferred_element_type=F32) + jnp.dot(xh, wu[dh:], preferred_element_type=F32))
        hid_scr[s] = (_silu(g) * u).astype(BF16)

    @pl.when(s >= nf)
    def _():
        issue_slice()
        wd = wd_ref[0, 0].astype(BF16)
        acc = jnp.dot(hid_scr[0], wd[:tf], preferred_element_type=F32)
        for k in range(1, nf):
            acc = acc + jnp.dot(hid_scr[k], wd[k * tf:(k + 1) * tf], preferred_element_type=F32)
        for grp in range(ngrp):
            o_ref[grp, 0] = acc[grp * cap:(grp + 1) * cap]

    @pl.when(jnp.logical_and(e == ne - 1, s == nstep - 1))
    def _():
        wait_gather()


def _ffn(idx_flat, hps, w_gate, w_up, w_down, layer, cap, tf, tn):
    ngrp = len(hps)
    n, dh = hps[0].shape
    _, ne, d, f = w_gate.shape
    nf = f // tf
    nn = d // tn
    m = ngrp * cap
    assert all(hp.shape[0] >= m for hp in hps)
    assert cap % (nf + nn) == 0
    kern = functools.partial(_ffn_kernel, ngrp=ngrp, cap=cap, nf=nf, tf=tf, dh=dh, rows_per_step=cap // (nf + nn))
    grid_spec = pltpu.PrefetchScalarGridSpec(
        num_scalar_prefetch=1,
        grid=(ne, nf + nn),
        in_specs=[pl.BlockSpec(memory_space=pl.ANY)] * ngrp + [
            pl.BlockSpec((1, 1, d, tf), lambda e, s, idx: (layer, e, 0, jnp.minimum(s, nf - 1))),
            pl.BlockSpec((1, 1, d, tf), lambda e, s, idx: (layer, e, 0, jnp.minimum(s, nf - 1))),
            pl.BlockSpec((1, 1, f, tn), lambda e, s, idx: (layer, e, 0, jnp.maximum(s - nf, 0))),
        ],
        out_specs=pl.BlockSpec((ngrp, 1, cap, tn), lambda e, s, idx: (0, e, 0, jnp.maximum(s - nf, 0))),
        scratch_shapes=[
            pltpu.VMEM((m, dh), I32),
            pltpu.VMEM((2, m, dh), BF16),
            pltpu.VMEM((nf, m, tf), BF16),
            pltpu.SemaphoreType.DMA(()),
        ],
    )
    return pl.pallas_call(
        kern,
        grid_spec=grid_spec,
        out_shape=jax.ShapeDtypeStruct((ngrp, ne, cap, d), F32),
        compiler_params=_cparams(("arbitrary", "arbitrary")),
        name="moe_ffn",
    )(idx_flat, *hps, w_gate, w_up, w_down)


def _worklist(off, ne, ntile, nblk):
    bs = SLOT_BLOCK
    w_max = ne * (nblk + ntile - 1) + ntile
    lo = off[:, :ntile].T.reshape(-1)
    hi = off[:, 1:ntile + 1].T.reshape(-1)
    b0 = jnp.minimum(lo // bs, nblk - 1)
    nb = jnp.where(hi > lo, (hi + bs - 1) // bs - b0, 0)
    first_e = (jnp.arange(ntile * ne, dtype=I32) % ne) == 0
    nb = jnp.where(first_e, jnp.maximum(nb, 1), nb)
    end = jnp.cumsum(nb)
    total = end[-1]
    w = jnp.arange(w_max, dtype=I32)
    wc = jnp.minimum(w, total - 1)
    pair = jnp.sum((end[None, :] <= wc[:, None]).astype(I32), axis=1)
    table = jnp.stack([end - nb, b0, lo, hi], axis=1)
    onehot = pair[:, None] == jnp.arange(ntile * ne, dtype=I32)[None, :]
    got = jnp.sum(jnp.where(onehot[:, :, None], table[None], 0), axis=1)
    tile = pair // ne
    exp = pair % ne
    blk = got[:, 1] + (wc - got[:, 0])
    valid = w < total
    rlo = jnp.where(valid, jnp.clip(got[:, 2] - blk * bs, 0, bs), 0)
    rhi = jnp.where(valid, jnp.clip(got[:, 3] - blk * bs, 0, bs), 0)
    first = jnp.concatenate([jnp.ones((1,), bool), tile[1:] != tile[:-1]])
    last = jnp.concatenate([tile[1:] != tile[:-1], jnp.ones((1,), bool)])
    to_i = lambda a: a.astype(I32)
    return to_i(tile), to_i(exp), to_i(blk), to_i(rlo), to_i(rhi), to_i(first), to_i(last)


def _combine_kernel(tile_sm, exp_sm, blk_sm, rlo_sm, rhi_sm, first_sm, last_sm, idx_sm, gate_sm,
                    src_ref, x_ref, g2_ref, fg_ref, o_ref, acc_scr, *, cap, tmtok, final):
    w = pl.program_id(0)

    @pl.when(first_sm[w] == 1)
    def _():
        acc_scr[...] = jnp.zeros_like(acc_scr)

    slot0 = exp_sm[w] * cap + blk_sm[w] * SLOT_BLOCK
    tok0 = tile_sm[w] * tmtok

    rlo = rlo_sm[w]
    rhi = rhi_sm[w]
    nquad = (rhi - rlo) // COMBINE_ROWS

    def quad(q, _):
        r0 = rlo + q * COMBINE_ROWS
        toks = [idx_sm[slot0 + r0 + u] - tok0 for u in range(COMBINE_ROWS)]
        new = [acc_scr[pl.ds(toks[u], 1), :] + src_ref[0, 0, pl.ds(r0 + u, 1), :] * gate_sm[slot0 + r0 + u]
               for u in range(COMBINE_ROWS)]
        for u in range(COMBINE_ROWS):
            acc_scr[pl.ds(toks[u], 1), :] = new[u]
        return 0

    lax.fori_loop(0, nquad, quad, 0)

    def row(r, _):
        tok = idx_sm[slot0 + r] - tok0
        acc_scr[pl.ds(tok, 1), :] = acc_scr[pl.ds(tok, 1), :] + src_ref[0, 0, pl.ds(r, 1), :] * gate_sm[slot0 + r]
        return 0

    lax.fori_loop(rlo + nquad * COMBINE_ROWS, rhi, row, 0)

    @pl.when(last_sm[w] == 1)
    def _():
        xn = x_ref[0] + g2_ref[0] * acc_scr[...]
        if final:
            ms = jnp.mean(xn * xn, axis=-1, keepdims=True)
            xn = xn * lax.rsqrt(ms + RMS_EPS) * fg_ref[...]
        o_ref[0] = xn


def _combine(work, idx_flat_g, gate_flat_g, outs, grp, x, mod, final_g, tmtok, final):
    b, l, d = x.shape
    ne, cap = outs.shape[1], outs.shape[2]
    tpb = l // tmtok
    w_max = work[0].shape[0]
    kern = functools.partial(_combine_kernel, cap=cap, tmtok=tmtok, final=final)
    nj = 1
    grid_spec = pltpu.PrefetchScalarGridSpec(
        num_scalar_prefetch=9,
        grid=(w_max,),
        in_specs=[
            pl.BlockSpec((1, 1, SLOT_BLOCK, d),
                         lambda w, ti, ex, bl, *_: (grp, ex[w], bl[w], 0)),
            pl.BlockSpec((1, tmtok, d), lambda w, ti, *_: (ti[w] // tpb, ti[w] % tpb, 0),
                         pipeline_mode=pl.Buffered(1)),
            pl.BlockSpec((1, 1, d), lambda w, ti, *_: (ti[w] // tpb, 0, 5 * nj)),
            pl.BlockSpec((1, d), lambda w, *_: (0, 0)),
        ],
        out_specs=pl.BlockSpec((1, tmtok, d), lambda w, ti, *_: (ti[w] // tpb, ti[w] % tpb, 0)),
        scratch_shapes=[pltpu.VMEM((tmtok, d), F32)],
    )
    return pl.pallas_call(
        kern,
        grid_spec=grid_spec,
        out_shape=jax.ShapeDtypeStruct((b, l, d), F32),
        compiler_params=_cparams(("arbitrary",)),
        name="moe_combine",
    )(*work, idx_flat_g, gate_flat_g, outs, x, mod, final_g.reshape(1, d))


def _pick(n, pref):
    t = min(pref, n)
    while n % t:
        t //= 2
    return t


def _moe_layer(xs, mods, norm_g, w_router, w_gate, w_up, w_down, layer, final_g, final):
    ne = w_router.shape[1]
    hps, idxs, gates, works = [], [], [], []
    caps = set()
    for x, mod in zip(xs, mods):
        b, l, d = x.shape
        n = b * l
        cap = EC_CAPACITY_FACTOR * n // ne
        caps.add(cap)
        tmtok = _pick(l, 1024)
        ntile = n // tmtok
        hp, aff_t = _router(x, norm_g, mod, w_router, _pick(l, 512))
        idx, gate, off = _select(aff_t, cap, ntile, tmtok)
        hps.append(hp.reshape(n, d // 2))
        idxs.append(idx.reshape(-1))
        gates.append(gate.reshape(-1))
        works.append(_worklist(off, ne, ntile, cap // SLOT_BLOCK))
    assert len(caps) == 1, "request groups must have equal expert capacity"
    cap = caps.pop()
    f = w_gate.shape[3]
    outs = _ffn(jnp.concatenate(idxs), hps, w_gate, w_up, w_down, layer, cap,
                _pick(f, 256), _pick(xs[0].shape[2], 256))
    new = []
    for grp, (x, mod) in enumerate(zip(xs, mods)):
        tmtok = _pick(x.shape[1], 1024)
        new.append(_combine(works[grp], idxs[grp], gates[grp], outs, grp, x, mod, final_g, tmtok, final))
    return new


def kernel(x_prompt, x_sample, c_prompt, c_sample, ada_w, ada_b, norm_mix_g, norm_ffn_g, final_norm_g, conv_w_in, conv_b_in, conv_w_dw, conv_b_dw, conv_ln_g, conv_ln_b, conv_w_out, conv_b_out, ssm_lambda_re, ssm_lambda_im, ssm_log_step, ssm_b_re, ssm_b_im, ssm_c_re, ssm_c_im, ssm_d, ssm_w_glu, ssm_b_glu, moe_w_router, moe_w_gate, moe_w_up, moe_w_down):
    depth = ada_w.shape[0]
    d = x_prompt.shape[-1]
    bp, bs = x_prompt.shape[0], x_sample.shape[0]
    nrow = 8 * ((bp + bs + 7) // 8)
    c_all = jnp.concatenate([c_prompt, c_sample, jnp.zeros((nrow - bp - bs, d), F32)], axis=0)
    mod_all = _ada(c_all, ada_w, ada_b)

    xs = [x_prompt, x_sample]
    for i in range(depth):
        mods = [mod_all[i, :bp].reshape(bp, 1, 6 * d), mod_all[i, bp:bp + bs].reshape(bs, 1, 6 * d)]
        j = i // 2
        if i % 2 == 0:
            w_in = conv_w_in[j].astype(BF16)
            w_out = conv_w_out[j].astype(BF16)
            nxt = []
            for x, mod in zip(xs, mods):
                l = x.shape[1]
                u = _conv_in(x, norm_mix_g[i], mod, w_in, conv_b_in[j], _pick(l, 512), _pick(d, 512))
                nxt.append(_conv_out(u, x, mod, conv_w_dw[j], conv_b_dw[j], conv_ln_g[j], conv_ln_b[j],
                                     w_out, conv_b_out[j], _pick(l, 512), _pick(d, 512)))
            xs = nxt
        else:
            lay = _s5_layout(ssm_lambda_re[j], ssm_lambda_im[j], ssm_log_step[j], ssm_b_re[j], ssm_b_im[j],
                             ssm_c_re[j], ssm_c_im[j])
            w_glu = ssm_w_glu[j].astype(BF16)
            nxt = []
            for x, mod in zip(xs, mods):
                l = x.shape[1]
                y2 = _s5_scan(x, norm_mix_g[i], mod, lay, _pick(l, 256))
                nxt.append(_s5_glu(y2, x, norm_mix_g[i], mod, ssm_d[j], w_glu, ssm_b_glu[j],
                                   _pick(l, 512), _pick(d, 512)))
            xs = nxt
        xs = _moe_layer(xs, mods, norm_ffn_g[i], moe_w_router[i], moe_w_gate, moe_w_up, moe_w_down, i,
                        final_norm_g, final=(i == depth - 1))
    return (xs[0], xs[1])
```

```python
import functools
import math

import jax
import jax.numpy as jnp
from jax import lax
from jax.experimental import pallas as pl
from jax.experimental.pallas import tpu as pltpu

F32 = jnp.float32
BF16 = jnp.bfloat16
I32 = jnp.int32
HIGHEST = lax.Precision.HIGHEST

RMS_EPS = 1e-6
LN_EPS = 1e-5
CONV_WIDTH = 31
CONV_PAD = (CONV_WIDTH - 1) // 2
CONV_HALO = 16
SSM_GROUP = 16
SSM_STATE = 64
SSM_BLOCKS = 8
S5_LAG = 4
N_EXPERTS = 16
EC_CAPACITY_FACTOR = 2
LANES = 128
SLOT_BLOCK = 128
COMBINE_ROWS = 8
ROW_CHUNK = 16

VMEM_LIMIT = 56 * 1024 * 1024


def _cparams(sem, vmem=VMEM_LIMIT):
    return pltpu.CompilerParams(dimension_semantics=sem, vmem_limit_bytes=vmem)


def _modnorm_rows(x_ref, g_ref, sh_ref, sc_ref, inv_scr, emit):
    x = x_ref[0]
    inv_scr[...] = lax.rsqrt(jnp.mean(x * x, axis=-1, keepdims=True) + RMS_EPS)
    gain = g_ref[...] * (1.0 + sc_ref[0])
    shift = sh_ref[0]

    def chunk(r, _):
        rows = pl.ds(pl.multiple_of(r * ROW_CHUNK, ROW_CHUNK), ROW_CHUNK)
        emit(rows, x_ref[0, rows, :] * inv_scr[rows, :] * gain + shift)
        return 0

    lax.fori_loop(0, x.shape[0] // ROW_CHUNK, chunk, 0, unroll=2)


def _silu(x):
    return x * jax.nn.sigmoid(x)


def _gelu_tanh(x):
    c = math.sqrt(2.0 / math.pi)
    return 0.5 * x * (1.0 + jnp.tanh(c * (x + 0.044715 * (x * x * x))))


def _ada_kernel(c_ref, w_ref, b_ref, o_ref):
    c = c_ref[...]
    cond = _silu(c)
    c_hi = cond.astype(BF16)
    c_lo = (cond - c_hi.astype(F32)).astype(BF16)
    w = w_ref[0]
    w_hi = w.astype(BF16)
    w_lo = (w - w_hi.astype(F32)).astype(BF16)
    o_ref[0] = (jnp.dot(c_hi, w_hi, preferred_element_type=F32) + jnp.dot(c_lo, w_hi, preferred_element_type=F32)
                + jnp.dot(c_hi, w_lo, preferred_element_type=F32) + b_ref[0])


def _ada(c_all, ada_w, ada_b):
    depth, d, n6 = ada_w.shape
    rows = c_all.shape[0]
    tn = _pick(n6, 1024)
    return pl.pallas_call(
        _ada_kernel,
        grid=(depth, n6 // tn),
        in_specs=[
            pl.BlockSpec((rows, d), lambda i, j: (0, 0)),
            pl.BlockSpec((1, d, tn), lambda i, j: (i, 0, j)),
            pl.BlockSpec((1, 1, tn), lambda i, j: (i, 0, j)),
        ],
        out_specs=pl.BlockSpec((1, rows, tn), lambda i, j: (i, 0, j)),
        out_shape=jax.ShapeDtypeStruct((depth, rows, n6), F32),
        compiler_params=_cparams(("parallel", "parallel")),
        name="ada_mod",
    )(c_all, ada_w, ada_b.reshape(depth, 1, n6))


def _conv_in_kernel(x_ref, g_ref, sh_ref, sc_ref, wa_ref, wg_ref, ba_ref, bg_ref, o_ref, h_scr, inv_scr):
    @pl.when(pl.program_id(2) == 0)
    def _():
        def emit(rows, h):
            h_scr[rows, :] = h.astype(BF16)

        _modnorm_rows(x_ref, g_ref, sh_ref, sc_ref, inv_scr, emit)

    h = h_scr[...]
    a = jnp.dot(h, wa_ref[...], preferred_element_type=F32) + ba_ref[...]
    g = jnp.dot(h, wg_ref[...], preferred_element_type=F32) + bg_ref[...]
    o_ref[0] = a * jax.nn.sigmoid(g)


def _conv_in(x, norm_g, mod, w_in_bf, b_in, tm, tn):
    b, l, d = x.shape
    nj = d // tn
    return pl.pallas_call(
        _conv_in_kernel,
        grid=(b, l // tm, nj),
        in_specs=[
            pl.BlockSpec((1, tm, d), lambda bi, i, j: (bi, i, 0)),
            pl.BlockSpec((1, d), lambda bi, i, j: (0, 0)),
            pl.BlockSpec((1, 1, d), lambda bi, i, j: (bi, 0, 0)),
            pl.BlockSpec((1, 1, d), lambda bi, i, j: (bi, 0, 1)),
            pl.BlockSpec((d, tn), lambda bi, i, j: (0, j)),
            pl.BlockSpec((d, tn), lambda bi, i, j: (0, j + nj)),
            pl.BlockSpec((1, tn), lambda bi, i, j: (0, j)),
            pl.BlockSpec((1, tn), lambda bi, i, j: (0, j + nj)),
        ],
        out_specs=pl.BlockSpec((1, tm, tn), lambda bi, i, j: (bi, i, j)),
        out_shape=jax.ShapeDtypeStruct((b, l, d), F32),
        scratch_shapes=[pltpu.VMEM((tm, d), BF16), pltpu.VMEM((tm, 1), F32)],
        compiler_params=_cparams(("parallel", "parallel", "arbitrary")),
        name="conv_in_glu",
    )(x, norm_g.reshape(1, d), mod, mod, w_in_bf, w_in_bf, b_in.reshape(1, 2 * d), b_in.reshape(1, 2 * d))


def _conv_out_kernel(u_ref, up_ref, un_ref, wdw_ref, bdw_ref, lg_ref, lb_ref, wo_ref, bo_ref, x_ref, g1_ref,
                     o_ref, ext_scr, c_scr, v_scr, mu_scr, inv_scr, *, tm, d, rc, cc):
    i = pl.program_id(1)
    ni = pl.num_programs(1)

    ncol = d // cc

    @pl.when(pl.program_id(2) == 0)
    def _():
        for c in range(ncol):
            cols = pl.ds(c * cc, cc)
            ext_scr[c, pl.ds(0, CONV_HALO), :] = jnp.where(i > 0, up_ref[0, :, cols], 0.0)
            ext_scr[c, pl.ds(CONV_HALO, tm), :] = u_ref[0, :, cols]
            ext_scr[c, pl.ds(CONV_HALO + tm, CONV_HALO), :] = jnp.where(i < ni - 1, un_ref[0, :, cols], 0.0)

        def col_chunk(c, _):
            for r0 in range(0, tm, rc):
                acc = jnp.zeros((rc, cc), F32) + bdw_ref[c]
                for k in range(CONV_WIDTH):
                    seg = ext_scr.at[c][pl.ds(r0 + (CONV_HALO - CONV_PAD + k), rc, stride=1), :]
                    acc = acc + seg * wdw_ref[c, pl.ds(k, 1), :]
                c_scr[c, pl.ds(r0, rc), :] = acc
            return 0

        lax.fori_loop(0, ncol, col_chunk, 0)

        def rows_of(r):
            return pl.ds(pl.multiple_of(r * ROW_CHUNK, ROW_CHUNK), ROW_CHUNK)

        def mean_chunk(r, _):
            rows = rows_of(r)
            tot = functools.reduce(lambda a, v: a + v, [c_scr[c, rows, :] for c in range(ncol)])
            mu_scr[rows, :] = jnp.sum(tot, axis=-1, keepdims=True) * (1.0 / d)
            return 0

        def var_chunk(r, _):
            rows = rows_of(r)
            mu = mu_scr[rows, :]
            dev = [c_scr[c, rows, :] - mu for c in range(ncol)]
            tot = functools.reduce(lambda a, v: a + v, [v * v for v in dev])
            inv_scr[rows, :] = lax.rsqrt(jnp.sum(tot, axis=-1, keepdims=True) * (1.0 / d) + LN_EPS)
            return 0

        def norm_chunk(r, _):
            rows = rows_of(r)
            mu = mu_scr[rows, :]
            inv = inv_scr[rows, :]
            for c in range(ncol):
                cols = pl.ds(c * cc, cc)
                y = (c_scr[c, rows, :] - mu) * inv * lg_ref[:, cols] + lb_ref[:, cols]
                v_scr[rows, cols] = _silu(y).astype(BF16)
            return 0

        for body, unroll in ((mean_chunk, 16), (var_chunk, 4), (norm_chunk, 4)):
            lax.fori_loop(0, tm // ROW_CHUNK, body, 0, unroll=min(unroll, tm // ROW_CHUNK))

    o = jnp.dot(v_scr[...], wo_ref[...], preferred_element_type=F32) + bo_ref[...]
    o_ref[0] = x_ref[0] + g1_ref[0] * o


def _conv_out(u, x, mod, w_dw, b_dw, ln_g, ln_b, w_out_bf, b_out, tm, tn):
    b, l, d = x.shape
    nj = d // tn
    hb = tm // CONV_HALO
    nhb = l // CONV_HALO
    rc = min(64, tm)
    cc = min(LANES, d)
    ncol = d // cc
    w_dw_p = jnp.concatenate([w_dw, jnp.zeros((1, d), w_dw.dtype)], axis=0)
    w_dw_p = w_dw_p.reshape(CONV_WIDTH + 1, ncol, cc).transpose(1, 0, 2)
    kern = functools.partial(_conv_out_kernel, tm=tm, d=d, rc=rc, cc=cc)
    return pl.pallas_call(
        kern,
        grid=(b, l // tm, nj),
        in_specs=[
            pl.BlockSpec((1, tm, d), lambda bi, i, j: (bi, i, 0)),
            pl.BlockSpec((1, CONV_HALO, d), lambda bi, i, j: (bi, jnp.maximum(i * hb - 1, 0), 0)),
            pl.BlockSpec((1, CONV_HALO, d), lambda bi, i, j: (bi, jnp.minimum((i + 1) * hb, nhb - 1), 0)),
            pl.BlockSpec((ncol, CONV_WIDTH + 1, cc), lambda bi, i, j: (0, 0, 0)),
            pl.BlockSpec((ncol, 1, cc), lambda bi, i, j: (0, 0, 0)),
            pl.BlockSpec((1, d), lambda bi, i, j: (0, 0)),
            pl.BlockSpec((1, d), lambda bi, i, j: (0, 0)),
            pl.BlockSpec((d, tn), lambda bi, i, j: (0, j)),
            pl.BlockSpec((1, tn), lambda bi, i, j: (0, j)),
            pl.BlockSpec((1, tm, tn), lambda bi, i, j: (bi, i, j)),
            pl.BlockSpec((1, 1, tn), lambda bi, i, j: (bi, 0, 2 * nj + j)),
        ],
        out_specs=pl.BlockSpec((1, tm, tn), lambda bi, i, j: (bi, i, j)),
        out_shape=jax.ShapeDtypeStruct((b, l, d), F32),
        scratch_shapes=[
            pltpu.VMEM((ncol, tm + 2 * CONV_HALO, cc), F32),
            pltpu.VMEM((ncol, tm, cc), F32),
            pltpu.VMEM((tm, d), BF16),
            pltpu.VMEM((tm, 1), F32),
            pltpu.VMEM((tm, 1), F32),
        ],
        compiler_params=_cparams(("parallel", "parallel", "arbitrary")),
        name="conv_dw_ln_out",
    )(u, u, u, w_dw_p, b_dw.reshape(ncol, 1, cc), ln_g.reshape(1, d), ln_b.reshape(1, d), w_out_bf,
      b_out.reshape(1, d), x, mod)


def _zoh_discretise(lam_re, lam_im, log_step, b_re, b_im):
    dt = jnp.exp(log_step)[..., None]
    mag = jnp.exp(lam_re * dt)
    a_re = mag * jnp.cos(lam_im * dt)
    a_im = mag * jnp.sin(lam_im * dt)
    nr = a_re - 1.0
    ni = a_im
    den = lam_re * lam_re + lam_im * lam_im
    k_re = (nr * lam_re + ni * lam_im) / den
    k_im = (ni * lam_re - nr * lam_im) / den
    bb_re = k_re[..., None] * b_re - k_im[..., None] * b_im
    bb_im = k_re[..., None] * b_im + k_im[..., None] * b_re
    return a_re, a_im, bb_re, bb_im


def _s5_layout(lam_re, lam_im, log_step, b_re, b_im, c_re, c_im):
    a_re, a_im, bb_re, bb_im = _zoh_discretise(lam_re, lam_im, log_step, b_re, b_im)
    nd, g, p = a_re.shape
    c = b_re.shape[-1]
    gpb = g // SSM_BLOCKS
    eye = jnp.eye(gpb, dtype=F32)

    def expand(bb):
        bb = bb.reshape(nd, SSM_BLOCKS, gpb, p, c)
        return jnp.einsum('dsgpc,gh->dsgchp', bb, eye).reshape(nd, SSM_BLOCKS, gpb * c, gpb * p).astype(BF16)

    def contract(cc):
        cc = cc.reshape(nd, SSM_BLOCKS, gpb, c, p)
        return jnp.einsum('dsgcp,gh->dshpgc', cc, eye).reshape(nd, SSM_BLOCKS, gpb * p, gpb * c).astype(BF16)

    return (a_re.reshape(nd, SSM_BLOCKS, gpb * p), a_im.reshape(nd, SSM_BLOCKS, gpb * p),
            expand(bb_re), expand(bb_im), contract(c_re), contract(c_im))


def _s5_kernel(x_ref, g_ref, sh_ref, sc_ref, ar_ref, ai_ref, wer_ref, wei_ref, wcr_ref, wci_ref,
               o_ref, sre, sim, cre, cim, h_scr, inv_scr, *, t, chb, nsb, pitch):
    dirn = pl.program_id(0)
    nslab = nsb // LANES

    @pl.when(jnp.logical_and(jnp.logical_and(dirn == 0, pl.program_id(1) == 0), pl.program_id(2) == 0))
    def _():
        sre[...] = jnp.zeros_like(sre)
        sim[...] = jnp.zeros_like(sim)

    @pl.when(pl.program_id(2) == 0)
    def _():
        cre[...] = jnp.zeros_like(cre)
        cim[...] = jnp.zeros_like(cim)

    def emit(rows, h):
        h_scr[rows, :] = h.astype(BF16)

    _modnorm_rows(x_ref, g_ref, sh_ref, sc_ref, inv_scr, emit)

    def base(s):
        return s * pitch + (S5_LAG if s % 2 else 0)

    for s in range(SSM_BLOCKS):
        hs = h_scr[:, s * chb:(s + 1) * chb]
        bur = jnp.dot(hs, wer_ref[0, s], preferred_element_type=F32)
        bui = jnp.dot(hs, wei_ref[0, s], preferred_element_type=F32)
        for j in range(nslab):
            sre[j, pl.ds(base(s), t), :] = bur[:, j * LANES:(j + 1) * LANES]
            sim[j, pl.ds(base(s), t), :] = bui[:, j * LANES:(j + 1) * LANES]

    ar = [ar_ref[0, :, j * LANES:(j + 1) * LANES] for j in range(nslab)]
    ai = [ai_ref[0, :, j * LANES:(j + 1) * LANES] for j in range(nslab)]
    odd = jnp.bitwise_and(lax.broadcasted_iota(I32, (SSM_BLOCKS, LANES), 0), 1)
    nrow = t + S5_LAG

    def make_step(active):
        def step(k, carry):
            xr, xi = carry
            row = jnp.where(dirn == 0, k, nrow - 1 - k)
            rows = pl.ds(row, SSM_BLOCKS, stride=pitch)
            nr, ni = [], []
            for j in range(nslab):
                br = sre.at[j][rows, :]
                bi = sim.at[j][rows, :]
                r = ar[j] * xr[j] - ai[j] * xi[j] + br
                im = ar[j] * xi[j] + ai[j] * xr[j] + bi
                sre.at[j][rows, :] = r
                sim.at[j][rows, :] = im
                if active is not None:
                    r = jnp.where(active, r, xr[j])
                    im = jnp.where(active, im, xi[j])
                nr.append(r)
                ni.append(im)
            return tuple(nr), tuple(ni)
        return step

    x0 = (tuple(cre[:, j * LANES:(j + 1) * LANES] for j in range(nslab)),
          tuple(cim[:, j * LANES:(j + 1) * LANES] for j in range(nslab)))
    x1 = lax.fori_loop(0, S5_LAG, make_step(odd == dirn), x0)
    x2 = lax.fori_loop(S5_LAG, t, make_step(None), x1, unroll=4)
    xr, xi = lax.fori_loop(t, nrow, make_step(odd != dirn), x2)
    for j in range(nslab):
        cre[:, j * LANES:(j + 1) * LANES] = xr[j]
        cim[:, j * LANES:(j + 1) * LANES] = xi[j]

    for s in range(SSM_BLOCKS):
        sr = jnp.concatenate([sre[j, pl.ds(base(s), t), :] for j in range(nslab)], axis=1).astype(BF16)
        si = jnp.concatenate([sim[j, pl.ds(base(s), t), :] for j in range(nslab)], axis=1).astype(BF16)
        y = (jnp.dot(sr, wcr_ref[0, s], preferred_element_type=F32)
             - jnp.dot(si, wci_ref[0, s], preferred_element_type=F32))
        o_ref[0, 0, :, s * chb:(s + 1) * chb] = y


def _s5_scan(x, norm_g, mod, lay, t):
    a_re, a_im, we_re, we_im, wc_re, wc_im = lay
    b, l, d = x.shape
    nd = a_re.shape[0]
    chb = d // SSM_BLOCKS
    nsb = a_re.shape[-1]
    nslab = nsb // LANES
    nt = l // t
    pitch = t + S5_LAG
    kern = functools.partial(_s5_kernel, t=t, chb=chb, nsb=nsb, pitch=pitch)

    def tile(dd, i):
        return i + dd * (nt - 1 - 2 * i)

    single = pl.Buffered(1)
    return pl.pallas_call(
        kern,
        grid=(nd, b, nt),
        in_specs=[
            pl.BlockSpec((1, t, d), lambda dd, bi, i: (bi, tile(dd, i), 0)),
            pl.BlockSpec((1, d), lambda dd, bi, i: (0, 0)),
            pl.BlockSpec((1, 1, d), lambda dd, bi, i: (bi, 0, 0)),
            pl.BlockSpec((1, 1, d), lambda dd, bi, i: (bi, 0, 1)),
            pl.BlockSpec((1, SSM_BLOCKS, nsb), lambda dd, bi, i: (dd, 0, 0)),
            pl.BlockSpec((1, SSM_BLOCKS, nsb), lambda dd, bi, i: (dd, 0, 0)),
            pl.BlockSpec((1, SSM_BLOCKS, chb, nsb), lambda dd, bi, i: (dd, 0, 0, 0), pipeline_mode=single),
            pl.BlockSpec((1, SSM_BLOCKS, chb, nsb), lambda dd, bi, i: (dd, 0, 0, 0), pipeline_mode=single),
            pl.BlockSpec((1, SSM_BLOCKS, nsb, chb), lambda dd, bi, i: (dd, 0, 0, 0), pipeline_mode=single),
            pl.BlockSpec((1, SSM_BLOCKS, nsb, chb), lambda dd, bi, i: (dd, 0, 0, 0), pipeline_mode=single),
        ],
        out_specs=pl.BlockSpec((1, 1, t, d), lambda dd, bi, i: (dd, bi, tile(dd, i), 0)),
        out_shape=jax.ShapeDtypeStruct((nd, b, l, d), F32),
        scratch_shapes=[
            pltpu.VMEM((nslab, SSM_BLOCKS * pitch, LANES), F32),
            pltpu.VMEM((nslab, SSM_BLOCKS * pitch, LANES), F32),
            pltpu.VMEM((SSM_BLOCKS, nsb), F32),
            pltpu.VMEM((SSM_BLOCKS, nsb), F32),
            pltpu.VMEM((t, d), BF16),
            pltpu.VMEM((t, 1), F32),
        ],
        compiler_params=_cparams(("arbitrary", "arbitrary", "arbitrary")),
        name="s5_scan",
    )(x, norm_g.reshape(1, d), mod, mod, a_re, a_im, we_re, we_im, wc_re, wc_im)


def _s5_glu_kernel(yf_ref, yb_ref, xf_ref, g_ref, sh_ref, sc_ref, dsk_ref, wa_ref, wg_ref, ba_ref, bg_ref,
                   x_ref, g1_ref, o_ref, v_scr, inv_scr):
    @pl.when(pl.program_id(2) == 0)
    def _():
        def emit(rows, h):
            y = yf_ref[0, 0, rows, :] + yb_ref[0, 0, rows, :] + dsk_ref[...] * h
            v_scr[rows, :] = _gelu_tanh(y).astype(BF16)

        _modnorm_rows(xf_ref, g_ref, sh_ref, sc_ref, inv_scr, emit)

    v = v_scr[...]
    a = jnp.dot(v, wa_ref[...], preferred_element_type=F32) + ba_ref[...]
    g = jnp.dot(v, wg_ref[...], preferred_element_type=F32) + bg_ref[...]
    o_ref[0] = x_ref[0] + g1_ref[0] * (a * jax.nn.sigmoid(g))


def _s5_glu(y2, x, norm_g, mod, d_skip, w_glu_bf, b_glu, tm, tn):
    b, l, d = x.shape
    nj = d // tn
    return pl.pallas_call(
        _s5_glu_kernel,
        grid=(b, l // tm, nj),
        in_specs=[
            pl.BlockSpec((1, 1, tm, d), lambda bi, i, j: (0, bi, i, 0)),
            pl.BlockSpec((1, 1, tm, d), lambda bi, i, j: (1, bi, i, 0)),
            pl.BlockSpec((1, tm, d), lambda bi, i, j: (bi, i, 0)),
            pl.BlockSpec((1, d), lambda bi, i, j: (0, 0)),
            pl.BlockSpec((1, 1, d), lambda bi, i, j: (bi, 0, 0)),
            pl.BlockSpec((1, 1, d), lambda bi, i, j: (bi, 0, 1)),
            pl.BlockSpec((1, d), lambda bi, i, j: (0, 0)),
            pl.BlockSpec((d, tn), lambda bi, i, j: (0, j)),
            pl.BlockSpec((d, tn), lambda bi, i, j: (0, j + nj)),
            pl.BlockSpec((1, tn), lambda bi, i, j: (0, j)),
            pl.BlockSpec((1, tn), lambda bi, i, j: (0, j + nj)),
            pl.BlockSpec((1, tm, tn), lambda bi, i, j: (bi, i, j)),
            pl.BlockSpec((1, 1, tn), lambda bi, i, j: (bi, 0, 2 * nj + j)),
        ],
        out_specs=pl.BlockSpec((1, tm, tn), lambda bi, i, j: (bi, i, j)),
        out_shape=jax.ShapeDtypeStruct((b, l, d), F32),
        scratch_shapes=[pltpu.VMEM((tm, d), BF16), pltpu.VMEM((tm, 1), F32)],
        compiler_params=_cparams(("parallel", "parallel", "arbitrary")),
        name="s5_gelu_glu",
    )(y2, y2, x, norm_g.reshape(1, d), mod, mod, d_skip.reshape(1, d), w_glu_bf, w_glu_bf,
      b_glu.reshape(1, 2 * d), b_glu.reshape(1, 2 * d), x, mod)


def _router_kernel(x_ref, g_ref, sh_ref, sc_ref, wr_ref, hp_ref, aff_ref, hhi_scr, hlo_scr, inv_scr, *, dh):
    def emit(rows, h):
        hb = h.astype(BF16)
        hbf = hb.astype(F32)
        hhi_scr[rows, :] = hb
        hlo_scr[rows, :] = (h - hbf).astype(BF16)
        bits = lax.bitcast_convert_type(hbf, I32)
        lo = lax.shift_right_logical(bits[:, :dh], 16)
        hi = jnp.bitwise_and(bits[:, dh:], jnp.int32(-65536))
        hp_ref[rows, :] = jnp.bitwise_or(hi, lo)

    _modnorm_rows(x_ref, g_ref, sh_ref, sc_ref, inv_scr, emit)

    w = wr_ref[...]
    w_hi = w.astype(BF16)
    w_lo = (w - w_hi.astype(F32)).astype(BF16)
    nt = (((1,), (1,)), ((), ()))
    h_hi = hhi_scr[...]
    logits = (lax.dot_general(w_hi, h_hi, nt, preferred_element_type=F32)
              + lax.dot_general(w_lo, h_hi, nt, preferred_element_type=F32)
              + lax.dot_general(w_hi, hlo_scr[...], nt, preferred_element_type=F32))
    m = jnp.max(logits, axis=0, keepdims=True)
    ex = jnp.exp(logits - m)
    aff_ref[...] = ex / jnp.sum(ex, axis=0, keepdims=True)


def _router(x, norm_g, mod, w_router, tm):
    b, l, d = x.shape
    e = w_router.shape[1]
    dh = d // 2
    nt = l // tm
    kern = functools.partial(_router_kernel, dh=dh)
    return pl.pallas_call(
        kern,
        grid=(b, nt),
        in_specs=[
            pl.BlockSpec((1, tm, d), lambda bi, i: (bi, i, 0)),
            pl.BlockSpec((1, d), lambda bi, i: (0, 0)),
            pl.BlockSpec((1, 1, d), lambda bi, i: (bi, 0, 3)),
            pl.BlockSpec((1, 1, d), lambda bi, i: (bi, 0, 4)),
            pl.BlockSpec((e, d), lambda bi, i: (0, 0)),
        ],
        out_specs=[
            pl.BlockSpec((tm, dh), lambda bi, i: (bi * nt + i, 0)),
            pl.BlockSpec((e, tm), lambda bi, i: (0, bi * nt + i)),
        ],
        out_shape=[jax.ShapeDtypeStruct((b * l, dh), I32), jax.ShapeDtypeStruct((e, b * l), F32)],
        scratch_shapes=[pltpu.VMEM((tm, d), BF16), pltpu.VMEM((tm, d), BF16), pltpu.VMEM((tm, 1), F32)],
        compiler_params=_cparams(("parallel", "parallel")),
        name="moe_router",
    )(x, norm_g.reshape(1, d), mod, mod, w_router.T)


def _excl_cumsum_lanes(x):
    n = x.shape[1]
    lane = lax.broadcasted_iota(I32, x.shape, 1)
    inc = x
    k = 1
    while k < n:
        inc = inc + jnp.where(lane >= k, pltpu.roll(inc, k, axis=1), 0)
        k *= 2
    return inc - x


def _select_kernel(aff_ref, idx_ref, gate_ref, off_ref, pos_scr, vals_scr, racc, off_sm, sem, *, cap, ntile, tmtok):
    e, n = aff_ref.shape
    key = lax.bitcast_convert_type(aff_ref[...], I32)

    def bit_step(it, prefix):
        cand = jnp.bitwise_or(prefix, jnp.left_shift(jnp.int32(1), 30 - it))
        cnt = jnp.sum((key >= cand).astype(I32), axis=1, keepdims=True)
        return jnp.where(cnt >= cap, cand, prefix)

    thr = lax.fori_loop(0, 31, bit_step, jnp.zeros((e, 1), I32))
    gt = key > thr
    eq = key == thr
    need = cap - jnp.sum(gt.astype(I32), axis=1, keepdims=True)
    sel = jnp.logical_or(gt, jnp.logical_and(eq, _excl_cumsum_lanes(eq.astype(I32)) < need))
    sel_i = sel.astype(I32)
    pos_scr[...] = jnp.where(sel, _excl_cumsum_lanes(sel_i), -1)

    tok = lax.broadcasted_iota(I32, (e, n), 1)
    lane = lax.broadcasted_iota(I32, off_ref.shape, 1)
    off = jnp.zeros(off_ref.shape, I32)
    for b in range(1, ntile + 1):
        cnt = jnp.sum(jnp.where(tok < b * tmtok, sel_i, 0), axis=1, keepdims=True)
        off = jnp.where(lane == b, cnt, off)
    off_ref[...] = off
    off_copy = pltpu.make_async_copy(off_ref, off_sm, sem)
    off_copy.start()

    tok1 = lax.broadcasted_iota(I32, vals_scr.shape, 1)
    row = lax.broadcasted_iota(I32, vals_scr.shape, 0)
    tok_hi = jnp.right_shift(tok1, 7).astype(F32)
    tok_lo = jnp.bitwise_and(tok1, 127).astype(F32)
    off_copy.wait()

    def expert(ei, _):
        g = aff_ref[pl.ds(ei, 1), :]
        g0 = g.astype(BF16).astype(F32)
        r1 = g - g0
        g1 = r1.astype(BF16).astype(F32)
        g2 = r1 - g1
        vals_scr[...] = jnp.where(row == 0, tok_hi, jnp.where(row == 1, tok_lo, jnp.where(
            row == 2, g0, jnp.where(row == 3, g1, jnp.where(row == 4, g2, 0.0))))).astype(BF16)

        def chunk(ci, _):
            s0 = ci * SLOT_BLOCK
            slot = s0 + lax.broadcasted_iota(I32, (SLOT_BLOCK, tmtok), 0)
            racc[...] = jnp.zeros_like(racc)
            for tb in range(ntile):
                @pl.when(jnp.logical_and(off_sm[ei, tb] < s0 + SLOT_BLOCK, off_sm[ei, tb + 1] > s0))
                def _():
                    cols = pl.ds(tb * tmtok, tmtok)
                    onehot = jnp.where(pos_scr[pl.ds(ei, 1), cols] == slot, 1.0, 0.0).astype(BF16)
                    racc[...] += lax.dot_general(vals_scr[:, cols], onehot, (((1,), (1,)), ((), ())),
                                                 preferred_element_type=F32)
            r = racc[...]
            idx_ref[ei, pl.ds(ci, 1), :] = (r[0:1] * 128.0 + r[1:2]).astype(I32)
            gate_ref[ei, pl.ds(ci, 1), :] = r[2:3] + r[3:4] + r[4:5]
            return 0

        lax.fori_loop(0, cap // SLOT_BLOCK, chunk, 0)
        return 0

    lax.fori_loop(0, e, expert, 0)


def _select(aff_t, cap, ntile, tmtok):
    e, n = aff_t.shape
    nc = cap // SLOT_BLOCK
    kern = functools.partial(_select_kernel, cap=cap, ntile=ntile, tmtok=tmtok)
    idx, gate, off = pl.pallas_call(
        kern,
        out_shape=[jax.ShapeDtypeStruct((e, nc, SLOT_BLOCK), I32),
                   jax.ShapeDtypeStruct((e, nc, SLOT_BLOCK), F32),
                   jax.ShapeDtypeStruct((e, LANES), I32)],
        scratch_shapes=[pltpu.VMEM((e, n), I32), pltpu.VMEM((16, n), BF16), pltpu.VMEM((16, SLOT_BLOCK), F32),
                        pltpu.SMEM((e, LANES), I32), pltpu.SemaphoreType.DMA(())],
        compiler_params=pltpu.CompilerParams(vmem_limit_bytes=VMEM_LIMIT),
        name="moe_select",
    )(aff_t)
    return idx.reshape(e, cap), gate.reshape(e, cap), off


def _ffn_kernel(idx_sm, *refs, ngrp, cap, nf, tf, dh, rows_per_step):
    hp_hbm = refs[:ngrp]
    wg_ref, wu_ref, wd_ref, o_ref, xp_scr, xs_scr, hid_scr, sem = refs[ngrp:]
    e = pl.program_id(0)
    s = pl.program_id(1)
    ne = pl.num_programs(0)
    m = ngrp * cap

    def issue_gather(expert):
        for grp in range(ngrp):
            base = (grp * ne + expert) * cap

            def body(c, _):
                pltpu.make_async_copy(hp_hbm[grp].at[pl.ds(idx_sm[base + c], 1), :],
                                      xp_scr.at[pl.ds(grp * cap + c, 1), :], sem).start()
                return 0

            lax.fori_loop(0, cap, body, 0, unroll=8)

    def wait_gather():
        pltpu.make_async_copy(hp_hbm[0].at[pl.ds(0, m), :], xp_scr, sem).wait()

    nstep = pl.num_programs(1)

    @pl.when(jnp.logical_and(e == 0, s == 0))
    def _():
        issue_gather(0)

    @pl.when(s == 0)
    def _():
        wait_gather()
        p = xp_scr[...]
        xs_scr[0] = lax.bitcast_convert_type(jnp.left_shift(p, 16), F32).astype(BF16)
        xs_scr[1] = lax.bitcast_convert_type(jnp.bitwise_and(p, jnp.int32(-65536)), F32).astype(BF16)

    nxt = jnp.minimum(e + 1, ne - 1)

    def issue_slice():
        for grp in range(ngrp):
            base = (grp * ne + nxt) * cap + s * rows_per_step
            for c in range(rows_per_step):
                pltpu.make_async_copy(hp_hbm[grp].at[pl.ds(idx_sm[base + c], 1), :],
                                      xp_scr.at[pl.ds(grp * cap + s * rows_per_step + c, 1), :], sem).start()

    @pl.when(s < nf)
    def _():
        issue_slice()
        wg = wg_ref[0, 0].astype(BF16)
        wu = wu_ref[0, 0].astype(BF16)
        xl = xs_scr[0]
        xh = xs_scr[1]
        g = (jnp.dot(xl, wg[:dh], preferred_element_type=F32) + jnp.dot(xh, wg[dh:], preferred_element_type=F32))
        u = (jnp.dot(xl, wu[:dh], preferred_element_type=F32) + jnp.dot(xh, wu[dh:], preferred_element_type=F32))
        hid_scr[s] = (_silu(g) * u).astype(BF16)

    @pl.when(s >= nf)
    def _():
        issue_slice()
        wd = wd_ref[0, 0].astype(BF16)
        acc = jnp.dot(hid_scr[0], wd[:tf], preferred_element_type=F32)
        for k in range(1, nf):
            acc = acc + jnp.dot(hid_scr[k], wd[k * tf:(k + 1) * tf], preferred_element_type=F32)
        for grp in range(ngrp):
            o_ref[grp, 0] = acc[grp * cap:(grp + 1) * cap]

    @pl.when(jnp.logical_and(e == ne - 1, s == nstep - 1))
    def _():
        wait_gather()


def _ffn(idx_flat, hps, w_gate, w_up, w_down, layer, cap, tf, tn):
    ngrp = len(hps)
    n, dh = hps[0].shape
    _, ne, d, f = w_gate.shape
    nf = f // tf
    nn = d // tn
    m = ngrp * cap
    assert all(hp.shape[0] >= m for hp in hps)
    assert cap % (nf + nn) == 0
    kern = functools.partial(_ffn_kernel, ngrp=ngrp, cap=cap, nf=nf, tf=tf, dh=dh, rows_per_step=cap // (nf + nn))
    grid_spec = pltpu.PrefetchScalarGridSpec(
        num_scalar_prefetch=1,
        grid=(ne, nf + nn),
        in_specs=[pl.BlockSpec(memory_space=pl.ANY)] * ngrp + [
            pl.BlockSpec((1, 1, d, tf), lambda e, s, idx: (layer, e, 0, jnp.minimum(s, nf - 1))),
            pl.BlockSpec((1, 1, d, tf), lambda e, s, idx: (layer, e, 0, jnp.minimum(s, nf - 1))),
            pl.BlockSpec((1, 1, f, tn), lambda e, s, idx: (layer, e, 0, jnp.maximum(s - nf, 0))),
        ],
        out_specs=pl.BlockSpec((ngrp, 1, cap, tn), lambda e, s, idx: (0, e, 0, jnp.maximum(s - nf, 0))),
        scratch_shapes=[
            pltpu.VMEM((m, dh), I32),
            pltpu.VMEM((2, m, dh), BF16),
            pltpu.VMEM((nf, m, tf), BF16),
            pltpu.SemaphoreType.DMA(()),
        ],
    )
    return pl.pallas_call(
        kern,
        grid_spec=grid_spec,
        out_shape=jax.ShapeDtypeStruct((ngrp, ne, cap, d), F32),
        compiler_params=_cparams(("arbitrary", "arbitrary")),
        name="moe_ffn",
    )(idx_flat, *hps, w_gate, w_up, w_down)


def _worklist(off, ne, ntile, nblk):
    bs = SLOT_BLOCK
    w_max = ne * (nblk + ntile - 1) + ntile
    lo = off[:, :ntile].T.reshape(-1)
    hi = off[:, 1:ntile + 1].T.reshape(-1)
    b0 = jnp.minimum(lo // bs, nblk - 1)
    nb = jnp.where(hi > lo, (hi + bs - 1) // bs - b0, 0)
    first_e = (jnp.arange(ntile * ne, dtype=I32) % ne) == 0
    nb = jnp.where(first_e, jnp.maximum(nb, 1), nb)
    end = jnp.cumsum(nb)
    total = end[-1]
    w = jnp.arange(w_max, dtype=I32)
    wc = jnp.minimum(w, total - 1)
    pair = jnp.sum((end[None, :] <= wc[:, None]).astype(I32), axis=1)
    table = jnp.stack([end - nb, b0, lo, hi], axis=1)
    onehot = pair[:, None] == jnp.arange(ntile * ne, dtype=I32)[None, :]
    got = jnp.sum(jnp.where(onehot[:, :, None], table[None], 0), axis=1)
    tile = pair // ne
    exp = pair % ne
    blk = got[:, 1] + (wc - got[:, 0])
    valid = w < total
    rlo = jnp.where(valid, jnp.clip(got[:, 2] - blk * bs, 0, bs), 0)
    rhi = jnp.where(valid, jnp.clip(got[:, 3] - blk * bs, 0, bs), 0)
    first = jnp.concatenate([jnp.ones((1,), bool), tile[1:] != tile[:-1]])
    last = jnp.concatenate([tile[1:] != tile[:-1], jnp.ones((1,), bool)])
    to_i = lambda a: a.astype(I32)
    return to_i(tile), to_i(exp), to_i(blk), to_i(rlo), to_i(rhi), to_i(first), to_i(last)


def _combine_kernel(tile_sm, exp_sm, blk_sm, rlo_sm, rhi_sm, first_sm, last_sm, idx_sm, gate_sm,
                    src_ref, x_ref, g2_ref, fg_ref, o_ref, acc_scr, *, cap, tmtok, final):
    w = pl.program_id(0)

    @pl.when(first_sm[w] == 1)
    def _():
        acc_scr[...] = jnp.zeros_like(acc_scr)

    slot0 = exp_sm[w] * cap + blk_sm[w] * SLOT_BLOCK
    tok0 = tile_sm[w] * tmtok

    rlo = rlo_sm[w]
    rhi = rhi_sm[w]
    body0 = jnp.minimum(((rlo + COMBINE_ROWS - 1) // COMBINE_ROWS) * COMBINE_ROWS, rhi)
    body1 = jnp.maximum((rhi // COMBINE_ROWS) * COMBINE_ROWS, body0)

    def row(r, _):
        tok = idx_sm[slot0 + r] - tok0
        acc_scr[pl.ds(tok, 1), :] = acc_scr[pl.ds(tok, 1), :] + src_ref[0, 0, pl.ds(r, 1), :] * gate_sm[slot0 + r]
        return 0

    def group(gi, _):
        r0 = pl.multiple_of(body0 + gi * COMBINE_ROWS, COMBINE_ROWS)
        toks = [idx_sm[slot0 + r0 + u] - tok0 for u in range(COMBINE_ROWS)]
        new = [acc_scr[pl.ds(toks[u], 1), :] + src_ref[0, 0, pl.ds(r0 + u, 1), :] * gate_sm[slot0 + r0 + u]
               for u in range(COMBINE_ROWS)]
        for u in range(COMBINE_ROWS):
            acc_scr[pl.ds(toks[u], 1), :] = new[u]
        return 0

    lax.fori_loop(rlo, body0, row, 0)
    lax.fori_loop(0, (body1 - body0) // COMBINE_ROWS, group, 0)
    lax.fori_loop(body1, rhi, row, 0)

    @pl.when(last_sm[w] == 1)
    def _():
        xn = x_ref[0] + g2_ref[0] * acc_scr[...]
        if final:
            ms = jnp.mean(xn * xn, axis=-1, keepdims=True)
            xn = xn * lax.rsqrt(ms + RMS_EPS) * fg_ref[...]
        o_ref[0] = xn


def _combine(work, idx_flat_g, gate_flat_g, outs, grp, x, mod, final_g, tmtok, final):
    b, l, d = x.shape
    ne, cap = outs.shape[1], outs.shape[2]
    tpb = l // tmtok
    w_max = work[0].shape[0]
    kern = functools.partial(_combine_kernel, cap=cap, tmtok=tmtok, final=final)
    nj = 1
    grid_spec = pltpu.PrefetchScalarGridSpec(
        num_scalar_prefetch=9,
        grid=(w_max,),
        in_specs=[
            pl.BlockSpec((1, 1, SLOT_BLOCK, d),
                         lambda w, ti, ex, bl, *_: (grp, ex[w], bl[w], 0)),
            pl.BlockSpec((1, tmtok, d), lambda w, ti, *_: (ti[w] // tpb, ti[w] % tpb, 0),
                         pipeline_mode=pl.Buffered(1)),
            pl.BlockSpec((1, 1, d), lambda w, ti, *_: (ti[w] // tpb, 0, 5 * nj)),
            pl.BlockSpec((1, d), lambda w, *_: (0, 0)),
        ],
        out_specs=pl.BlockSpec((1, tmtok, d), lambda w, ti, *_: (ti[w] // tpb, ti[w] % tpb, 0)),
        scratch_shapes=[pltpu.VMEM((tmtok, d), F32)],
    )
    return pl.pallas_call(
        kern,
        grid_spec=grid_spec,
        out_shape=jax.ShapeDtypeStruct((b, l, d), F32),
        compiler_params=_cparams(("arbitrary",)),
        name="moe_combine",
    )(*work, idx_flat_g, gate_flat_g, outs, x, mod, final_g.reshape(1, d))


def _pick(n, pref):
    t = min(pref, n)
    while n % t:
        t //= 2
    return t


def _moe_layer(xs, mods, norm_g, w_router, w_gate, w_up, w_down, layer, final_g, final):
    ne = w_router.shape[1]
    hps, idxs, gates, works = [], [], [], []
    caps = set()
    for x, mod in zip(xs, mods):
        b, l, d = x.shape
        n = b * l
        cap = EC_CAPACITY_FACTOR * n // ne
        caps.add(cap)
        tmtok = _pick(l, 1024)
        ntile = n // tmtok
        hp, aff_t = _router(x, norm_g, mod, w_router, _pick(l, 512))
        idx, gate, off = _select(aff_t, cap, ntile, tmtok)
        hps.append(hp)
        idxs.append(idx.reshape(-1))
        gates.append(gate.reshape(-1))
        works.append(_worklist(off, ne, ntile, cap // SLOT_BLOCK))
    assert len(caps) == 1, "request groups must have equal expert capacity"
    cap = caps.pop()
    f = w_gate.shape[3]
    outs = _ffn(jnp.concatenate(idxs), hps, w_gate, w_up, w_down, layer, cap,
                _pick(f, 256), _pick(xs[0].shape[2], 256))
    new = []
    for grp, (x, mod) in enumerate(zip(xs, mods)):
        tmtok = _pick(x.shape[1], 1024)
        new.append(_combine(works[grp], idxs[grp], gates[grp], outs, grp, x, mod, final_g, tmtok, final))
    return new


def kernel(x_prompt, x_sample, c_prompt, c_sample, ada_w, ada_b, norm_mix_g, norm_ffn_g, final_norm_g, conv_w_in, conv_b_in, conv_w_dw, conv_b_dw, conv_ln_g, conv_ln_b, conv_w_out, conv_b_out, ssm_lambda_re, ssm_lambda_im, ssm_log_step, ssm_b_re, ssm_b_im, ssm_c_re, ssm_c_im, ssm_d, ssm_w_glu, ssm_b_glu, moe_w_router, moe_w_gate, moe_w_up, moe_w_down):
    depth = ada_w.shape[0]
    d = x_prompt.shape[-1]
    bp, bs = x_prompt.shape[0], x_sample.shape[0]
    nrow = 16 * ((bp + bs + 15) // 16)
    c_all = jnp.concatenate([c_prompt, c_sample, jnp.zeros((nrow - bp - bs, d), F32)], axis=0)
    mod_all = _ada(c_all, ada_w, ada_b)

    xs = [x_prompt, x_sample]
    for i in range(depth):
        mods = [mod_all[i, :bp].reshape(bp, 1, 6 * d), mod_all[i, bp:bp + bs].reshape(bs, 1, 6 * d)]
        j = i // 2
        if i % 2 == 0:
            w_in = conv_w_in[j].astype(BF16)
            w_out = conv_w_out[j].astype(BF16)
            nxt = []
            for x, mod in zip(xs, mods):
                l = x.shape[1]
                u = _conv_in(x, norm_mix_g[i], mod, w_in, conv_b_in[j], _pick(l, 512), _pick(d, 512))
                nxt.append(_conv_out(u, x, mod, conv_w_dw[j], conv_b_dw[j], conv_ln_g[j], conv_ln_b[j],
                                     w_out, conv_b_out[j], _pick(l, 512), _pick(d, 512)))
            xs = nxt
        else:
            lay = _s5_layout(ssm_lambda_re[j], ssm_lambda_im[j], ssm_log_step[j], ssm_b_re[j], ssm_b_im[j],
                             ssm_c_re[j], ssm_c_im[j])
            w_glu = ssm_w_glu[j].astype(BF16)
            nxt = []
            for x, mod in zip(xs, mods):
                l = x.shape[1]
                y2 = _s5_scan(x, norm_mix_g[i], mod, lay, _pick(l, 256))
                nxt.append(_s5_glu(y2, x, norm_mix_g[i], mod, ssm_d[j], w_glu, ssm_b_glu[j],
                                   _pick(l, 512), _pick(d, 512)))
            xs = nxt
        xs = _moe_layer(xs, mods, norm_ffn_g[i], moe_w_router[i], moe_w_gate, moe_w_up, moe_w_down, i,
                        final_norm_g, final=(i == depth - 1))
    return (xs[0], xs[1])
```

```python
import functools
import math

import jax
import jax.numpy as jnp
from jax import lax
from jax.experimental import pallas as pl
from jax.experimental.pallas import tpu as pltpu

F32 = jnp.float32
BF16 = jnp.bfloat16
I32 = jnp.int32
HIGHEST = lax.Precision.HIGHEST

RMS_EPS = 1e-6
LN_EPS = 1e-5
CONV_WIDTH = 31
CONV_PAD = (CONV_WIDTH - 1) // 2
CONV_HALO = 16
SSM_GROUP = 16
SSM_STATE = 64
SSM_BLOCKS = 8
S5_LAG = 4
N_EXPERTS = 16
EC_CAPACITY_FACTOR = 2
LANES = 128
SLOT_BLOCK = 128
COMBINE_ROWS = 8
ROW_CHUNK = 16

VMEM_LIMIT = 56 * 1024 * 1024


def _cparams(sem, vmem=VMEM_LIMIT):
    return pltpu.CompilerParams(dimension_semantics=sem, vmem_limit_bytes=vmem)


def _modnorm_rows(x_ref, g_ref, sh_ref, sc_ref, inv_scr, emit):
    x = x_ref[0]
    inv_scr[...] = lax.rsqrt(jnp.mean(x * x, axis=-1, keepdims=True) + RMS_EPS)
    gain = g_ref[...] * (1.0 + sc_ref[0])
    shift = sh_ref[0]

    def chunk(r, _):
        rows = pl.ds(pl.multiple_of(r * ROW_CHUNK, ROW_CHUNK), ROW_CHUNK)
        emit(rows, x_ref[0, rows, :] * inv_scr[rows, :] * gain + shift)
        return 0

    lax.fori_loop(0, x.shape[0] // ROW_CHUNK, chunk, 0, unroll=2)


def _silu(x):
    return x * jax.nn.sigmoid(x)


def _gelu_tanh(x):
    c = math.sqrt(2.0 / math.pi)
    return 0.5 * x * (1.0 + jnp.tanh(c * (x + 0.044715 * (x * x * x))))


def _ada_kernel(c_ref, w_ref, b_ref, o_ref):
    c = c_ref[...]
    cond = _silu(c)
    c_hi = cond.astype(BF16)
    c_lo = (cond - c_hi.astype(F32)).astype(BF16)
    w = w_ref[0]
    w_hi = w.astype(BF16)
    w_lo = (w - w_hi.astype(F32)).astype(BF16)
    o_ref[0] = (jnp.dot(c_hi, w_hi, preferred_element_type=F32) + jnp.dot(c_lo, w_hi, preferred_element_type=F32)
                + jnp.dot(c_hi, w_lo, preferred_element_type=F32) + b_ref[0])


def _ada(c_all, ada_w, ada_b):
    depth, d, n6 = ada_w.shape
    rows = c_all.shape[0]
    tn = _pick(n6, 1024)
    return pl.pallas_call(
        _ada_kernel,
        grid=(depth, n6 // tn),
        in_specs=[
            pl.BlockSpec((rows, d), lambda i, j: (0, 0)),
            pl.BlockSpec((1, d, tn), lambda i, j: (i, 0, j)),
            pl.BlockSpec((1, 1, tn), lambda i, j: (i, 0, j)),
        ],
        out_specs=pl.BlockSpec((1, rows, tn), lambda i, j: (i, 0, j)),
        out_shape=jax.ShapeDtypeStruct((depth, rows, n6), F32),
        compiler_params=_cparams(("parallel", "parallel")),
        name="ada_mod",
    )(c_all, ada_w, ada_b.reshape(depth, 1, n6))


def _conv_in_kernel(x_ref, g_ref, sh_ref, sc_ref, wa_ref, wg_ref, ba_ref, bg_ref, o_ref, h_scr, inv_scr):
    @pl.when(pl.program_id(2) == 0)
    def _():
        def emit(rows, h):
            h_scr[rows, :] = h.astype(BF16)

        _modnorm_rows(x_ref, g_ref, sh_ref, sc_ref, inv_scr, emit)

    h = h_scr[...]
    a = jnp.dot(h, wa_ref[...], preferred_element_type=F32) + ba_ref[...]
    g = jnp.dot(h, wg_ref[...], preferred_element_type=F32) + bg_ref[...]
    o_ref[0] = a * jax.nn.sigmoid(g)


def _conv_in(x, norm_g, mod, w_in_bf, b_in, tm, tn):
    b, l, d = x.shape
    nj = d // tn
    return pl.pallas_call(
        _conv_in_kernel,
        grid=(b, l // tm, nj),
        in_specs=[
            pl.BlockSpec((1, tm, d), lambda bi, i, j: (bi, i, 0)),
            pl.BlockSpec((1, d), lambda bi, i, j: (0, 0)),
            pl.BlockSpec((1, 1, d), lambda bi, i, j: (bi, 0, 0)),
            pl.BlockSpec((1, 1, d), lambda bi, i, j: (bi, 0, 1)),
            pl.BlockSpec((d, tn), lambda bi, i, j: (0, j)),
            pl.BlockSpec((d, tn), lambda bi, i, j: (0, j + nj)),
            pl.BlockSpec((1, tn), lambda bi, i, j: (0, j)),
            pl.BlockSpec((1, tn), lambda bi, i, j: (0, j + nj)),
        ],
        out_specs=pl.BlockSpec((1, tm, tn), lambda bi, i, j: (bi, i, j)),
        out_shape=jax.ShapeDtypeStruct((b, l, d), F32),
        scratch_shapes=[pltpu.VMEM((tm, d), BF16), pltpu.VMEM((tm, 1), F32)],
        compiler_params=_cparams(("parallel", "parallel", "arbitrary")),
        name="conv_in_glu",
    )(x, norm_g.reshape(1, d), mod, mod, w_in_bf, w_in_bf, b_in.reshape(1, 2 * d), b_in.reshape(1, 2 * d))


def _conv_out_kernel(u_ref, up_ref, un_ref, wdw_ref, bdw_ref, lg_ref, lb_ref, wo_ref, bo_ref, x_ref, g1_ref,
                     o_ref, ext_scr, c_scr, v_scr, mu_scr, inv_scr, *, tm, d, rc, cc):
    i = pl.program_id(1)
    ni = pl.num_programs(1)

    ncol = d // cc

    @pl.when(pl.program_id(2) == 0)
    def _():
        for c in range(ncol):
            cols = pl.ds(c * cc, cc)
            ext_scr[c, pl.ds(0, CONV_HALO), :] = jnp.where(i > 0, up_ref[0, :, cols], 0.0)
            ext_scr[c, pl.ds(CONV_HALO, tm), :] = u_ref[0, :, cols]
            ext_scr[c, pl.ds(CONV_HALO + tm, CONV_HALO), :] = jnp.where(i < ni - 1, un_ref[0, :, cols], 0.0)

        def col_chunk(c, _):
            for r0 in range(0, tm, rc):
                acc = jnp.zeros((rc, cc), F32) + bdw_ref[c]
                for k in range(CONV_WIDTH):
                    seg = ext_scr.at[c][pl.ds(r0 + (CONV_HALO - CONV_PAD + k), rc, stride=1), :]
                    acc = acc + seg * wdw_ref[c, pl.ds(k, 1), :]
                c_scr[c, pl.ds(r0, rc), :] = acc
            return 0

        lax.fori_loop(0, ncol, col_chunk, 0)

        def rows_of(r):
            return pl.ds(pl.multiple_of(r * ROW_CHUNK, ROW_CHUNK), ROW_CHUNK)

        def mean_chunk(r, _):
            rows = rows_of(r)
            tot = functools.reduce(lambda a, v: a + v, [c_scr[c, rows, :] for c in range(ncol)])
            mu_scr[rows, :] = jnp.sum(tot, axis=-1, keepdims=True) * (1.0 / d)
            return 0

        def var_chunk(r, _):
            rows = rows_of(r)
            mu = mu_scr[rows, :]
            dev = [c_scr[c, rows, :] - mu for c in range(ncol)]
            tot = functools.reduce(lambda a, v: a + v, [v * v for v in dev])
            inv_scr[rows, :] = lax.rsqrt(jnp.sum(tot, axis=-1, keepdims=True) * (1.0 / d) + LN_EPS)
            return 0

        def norm_chunk(r, _):
            rows = rows_of(r)
            mu = mu_scr[rows, :]
            inv = inv_scr[rows, :]
            for c in range(ncol):
                cols = pl.ds(c * cc, cc)
                y = (c_scr[c, rows, :] - mu) * inv * lg_ref[:, cols] + lb_ref[:, cols]
                v_scr[rows, cols] = _silu(y).astype(BF16)
            return 0

        for body, unroll in ((mean_chunk, 16), (var_chunk, 4), (norm_chunk, 4)):
            lax.fori_loop(0, tm // ROW_CHUNK, body, 0, unroll=min(unroll, tm // ROW_CHUNK))

    o = jnp.dot(v_scr[...], wo_ref[...], preferred_element_type=F32) + bo_ref[...]
    o_ref[0] = x_ref[0] + g1_ref[0] * o


def _conv_out(u, x, mod, w_dw, b_dw, ln_g, ln_b, w_out_bf, b_out, tm, tn):
    b, l, d = x.shape
    nj = d // tn
    hb = tm // CONV_HALO
    nhb = l // CONV_HALO
    rc = min(64, tm)
    cc = min(LANES, d)
    ncol = d // cc
    w_dw_p = jnp.concatenate([w_dw, jnp.zeros((1, d), w_dw.dtype)], axis=0)
    w_dw_p = w_dw_p.reshape(CONV_WIDTH + 1, ncol, cc).transpose(1, 0, 2)
    kern = functools.partial(_conv_out_kernel, tm=tm, d=d, rc=rc, cc=cc)
    return pl.pallas_call(
        kern,
        grid=(b, l // tm, nj),
        in_specs=[
            pl.BlockSpec((1, tm, d), lambda bi, i, j: (bi, i, 0)),
            pl.BlockSpec((1, CONV_HALO, d), lambda bi, i, j: (bi, jnp.maximum(i * hb - 1, 0), 0)),
            pl.BlockSpec((1, CONV_HALO, d), lambda bi, i, j: (bi, jnp.minimum((i + 1) * hb, nhb - 1), 0)),
            pl.BlockSpec((ncol, CONV_WIDTH + 1, cc), lambda bi, i, j: (0, 0, 0)),
            pl.BlockSpec((ncol, 1, cc), lambda bi, i, j: (0, 0, 0)),
            pl.BlockSpec((1, d), lambda bi, i, j: (0, 0)),
            pl.BlockSpec((1, d), lambda bi, i, j: (0, 0)),
            pl.BlockSpec((d, tn), lambda bi, i, j: (0, j)),
            pl.BlockSpec((1, tn), lambda bi, i, j: (0, j)),
            pl.BlockSpec((1, tm, tn), lambda bi, i, j: (bi, i, j)),
            pl.BlockSpec((1, 1, tn), lambda bi, i, j: (bi, 0, 2 * nj + j)),
        ],
        out_specs=pl.BlockSpec((1, tm, tn), lambda bi, i, j: (bi, i, j)),
        out_shape=jax.ShapeDtypeStruct((b, l, d), F32),
        scratch_shapes=[
            pltpu.VMEM((ncol, tm + 2 * CONV_HALO, cc), F32),
            pltpu.VMEM((ncol, tm, cc), F32),
            pltpu.VMEM((tm, d), BF16),
            pltpu.VMEM((tm, 1), F32),
            pltpu.VMEM((tm, 1), F32),
        ],
        compiler_params=_cparams(("parallel", "parallel", "arbitrary")),
        name="conv_dw_ln_out",
    )(u, u, u, w_dw_p, b_dw.reshape(ncol, 1, cc), ln_g.reshape(1, d), ln_b.reshape(1, d), w_out_bf,
      b_out.reshape(1, d), x, mod)


def _zoh_discretise(lam_re, lam_im, log_step, b_re, b_im):
    dt = jnp.exp(log_step)[..., None]
    mag = jnp.exp(lam_re * dt)
    a_re = mag * jnp.cos(lam_im * dt)
    a_im = mag * jnp.sin(lam_im * dt)
    nr = a_re - 1.0
    ni = a_im
    den = lam_re * lam_re + lam_im * lam_im
    k_re = (nr * lam_re + ni * lam_im) / den
    k_im = (ni * lam_re - nr * lam_im) / den
    bb_re = k_re[..., None] * b_re - k_im[..., None] * b_im
    bb_im = k_re[..., None] * b_im + k_im[..., None] * b_re
    return a_re, a_im, bb_re, bb_im


def _s5_layout(lam_re, lam_im, log_step, b_re, b_im, c_re, c_im):
    a_re, a_im, bb_re, bb_im = _zoh_discretise(lam_re, lam_im, log_step, b_re, b_im)
    nd, g, p = a_re.shape
    c = b_re.shape[-1]
    gpb = g // SSM_BLOCKS
    chb, nsb = gpb * c, gpb * p

    def rows_gc(bb):
        return bb.reshape(nd, SSM_BLOCKS, gpb, p, c).transpose(0, 1, 2, 4, 3).reshape(nd, SSM_BLOCKS, chb, p)

    def rows_gp(cc):
        return cc.reshape(nd, SSM_BLOCKS, gpb, c, p).transpose(0, 1, 2, 4, 3).reshape(nd, SSM_BLOCKS, nsb, c)

    def kern(er_ref, ei_ref, cr_ref, ci_ref, wer_ref, wei_ref, wcr_ref, wci_ref):
        def spread(x, width, rows_per_group, cols_per_group):
            k = x.shape[1]
            rep = (lax.broadcasted_iota(I32, (k, width), 1) % k == lax.broadcasted_iota(I32, (k, width), 0))
            full = jnp.dot(x.astype(BF16), rep.astype(BF16), preferred_element_type=F32)
            rg = lax.broadcasted_iota(I32, full.shape, 0) // rows_per_group
            cg = lax.broadcasted_iota(I32, full.shape, 1) // cols_per_group
            return jnp.where(rg == cg, full, 0.0).astype(BF16)

        wer_ref[0, 0] = spread(er_ref[0, 0], nsb, c, p)
        wei_ref[0, 0] = spread(ei_ref[0, 0], nsb, c, p)
        wcr_ref[0, 0] = spread(cr_ref[0, 0], chb, p, c)
        wci_ref[0, 0] = spread(ci_ref[0, 0], chb, p, c)

    blk = lambda r, k: pl.BlockSpec((1, 1, r, k), lambda dd, s: (dd, s, 0, 0))
    we_re, we_im, wc_re, wc_im = pl.pallas_call(
        kern,
        grid=(nd, SSM_BLOCKS),
        in_specs=[blk(chb, p), blk(chb, p), blk(nsb, c), blk(nsb, c)],
        out_specs=[blk(chb, nsb), blk(chb, nsb), blk(nsb, chb), blk(nsb, chb)],
        out_shape=[jax.ShapeDtypeStruct((nd, SSM_BLOCKS, chb, nsb), BF16)] * 2
        + [jax.ShapeDtypeStruct((nd, SSM_BLOCKS, nsb, chb), BF16)] * 2,
        compiler_params=_cparams(("parallel", "parallel")),
        name="s5_weight_layout",
    )(rows_gc(bb_re), rows_gc(bb_im), rows_gp(c_re), rows_gp(c_im))
    return (a_re.reshape(nd, SSM_BLOCKS, nsb), a_im.reshape(nd, SSM_BLOCKS, nsb), we_re, we_im, wc_re, wc_im)


def _s5_kernel(x_ref, g_ref, sh_ref, sc_ref, ar_ref, ai_ref, wer_ref, wei_ref, wcr_ref, wci_ref,
               o_ref, sre, sim, cre, cim, h_scr, inv_scr, *, t, chb, nsb, pitch):
    dirn = pl.program_id(0)
    nslab = nsb // LANES

    @pl.when(jnp.logical_and(jnp.logical_and(dirn == 0, pl.program_id(1) == 0), pl.program_id(2) == 0))
    def _():
        sre[...] = jnp.zeros_like(sre)
        sim[...] = jnp.zeros_like(sim)

    @pl.when(pl.program_id(2) == 0)
    def _():
        cre[...] = jnp.zeros_like(cre)
        cim[...] = jnp.zeros_like(cim)

    def emit(rows, h):
        h_scr[rows, :] = h.astype(BF16)

    _modnorm_rows(x_ref, g_ref, sh_ref, sc_ref, inv_scr, emit)

    def base(s):
        return s * pitch + (S5_LAG if s % 2 else 0)

    for s in range(SSM_BLOCKS):
        hs = h_scr[:, s * chb:(s + 1) * chb]
        bur = jnp.dot(hs, wer_ref[0, s], preferred_element_type=F32)
        bui = jnp.dot(hs, wei_ref[0, s], preferred_element_type=F32)
        for j in range(nslab):
            sre[j, pl.ds(base(s), t), :] = bur[:, j * LANES:(j + 1) * LANES]
            sim[j, pl.ds(base(s), t), :] = bui[:, j * LANES:(j + 1) * LANES]

    ar = [ar_ref[0, :, j * LANES:(j + 1) * LANES] for j in range(nslab)]
    ai = [ai_ref[0, :, j * LANES:(j + 1) * LANES] for j in range(nslab)]
    odd = jnp.bitwise_and(lax.broadcasted_iota(I32, (SSM_BLOCKS, LANES), 0), 1)
    nrow = t + S5_LAG

    def make_step(active):
        def step(k, carry):
            xr, xi = carry
            row = jnp.where(dirn == 0, k, nrow - 1 - k)
            rows = pl.ds(row, SSM_BLOCKS, stride=pitch)
            nr, ni = [], []
            for j in range(nslab):
                br = sre.at[j][rows, :]
                bi = sim.at[j][rows, :]
                r = ar[j] * xr[j] - ai[j] * xi[j] + br
                im = ar[j] * xi[j] + ai[j] * xr[j] + bi
                sre.at[j][rows, :] = r
                sim.at[j][rows, :] = im
                if active is not None:
                    r = jnp.where(active, r, xr[j])
                    im = jnp.where(active, im, xi[j])
                nr.append(r)
                ni.append(im)
            return tuple(nr), tuple(ni)
        return step

    x0 = (tuple(cre[:, j * LANES:(j + 1) * LANES] for j in range(nslab)),
          tuple(cim[:, j * LANES:(j + 1) * LANES] for j in range(nslab)))
    x1 = lax.fori_loop(0, S5_LAG, make_step(odd == dirn), x0)
    x2 = lax.fori_loop(S5_LAG, t, make_step(None), x1, unroll=4)
    xr, xi = lax.fori_loop(t, nrow, make_step(odd != dirn), x2)
    for j in range(nslab):
        cre[:, j * LANES:(j + 1) * LANES] = xr[j]
        cim[:, j * LANES:(j + 1) * LANES] = xi[j]

    for s in range(SSM_BLOCKS):
        sr = jnp.concatenate([sre[j, pl.ds(base(s), t), :] for j in range(nslab)], axis=1).astype(BF16)
        si = jnp.concatenate([sim[j, pl.ds(base(s), t), :] for j in range(nslab)], axis=1).astype(BF16)
        y = (jnp.dot(sr, wcr_ref[0, s], preferred_element_type=F32)
             - jnp.dot(si, wci_ref[0, s], preferred_element_type=F32))
        o_ref[0, 0, :, s * chb:(s + 1) * chb] = y


def _s5_scan(x, norm_g, mod, lay, t):
    a_re, a_im, we_re, we_im, wc_re, wc_im = lay
    b, l, d = x.shape
    nd = a_re.shape[0]
    chb = d // SSM_BLOCKS
    nsb = a_re.shape[-1]
    nslab = nsb // LANES
    nt = l // t
    pitch = t + S5_LAG
    kern = functools.partial(_s5_kernel, t=t, chb=chb, nsb=nsb, pitch=pitch)

    def tile(dd, i):
        return i + dd * (nt - 1 - 2 * i)

    single = pl.Buffered(1)
    return pl.pallas_call(
        kern,
        grid=(nd, b, nt),
        in_specs=[
            pl.BlockSpec((1, t, d), lambda dd, bi, i: (bi, tile(dd, i), 0)),
            pl.BlockSpec((1, d), lambda dd, bi, i: (0, 0)),
            pl.BlockSpec((1, 1, d), lambda dd, bi, i: (bi, 0, 0)),
            pl.BlockSpec((1, 1, d), lambda dd, bi, i: (bi, 0, 1)),
            pl.BlockSpec((1, SSM_BLOCKS, nsb), lambda dd, bi, i: (dd, 0, 0)),
            pl.BlockSpec((1, SSM_BLOCKS, nsb), lambda dd, bi, i: (dd, 0, 0)),
            pl.BlockSpec((1, SSM_BLOCKS, chb, nsb), lambda dd, bi, i: (dd, 0, 0, 0), pipeline_mode=single),
            pl.BlockSpec((1, SSM_BLOCKS, chb, nsb), lambda dd, bi, i: (dd, 0, 0, 0), pipeline_mode=single),
            pl.BlockSpec((1, SSM_BLOCKS, nsb, chb), lambda dd, bi, i: (dd, 0, 0, 0), pipeline_mode=single),
            pl.BlockSpec((1, SSM_BLOCKS, nsb, chb), lambda dd, bi, i: (dd, 0, 0, 0), pipeline_mode=single),
        ],
        out_specs=pl.BlockSpec((1, 1, t, d), lambda dd, bi, i: (dd, bi, tile(dd, i), 0)),
        out_shape=jax.ShapeDtypeStruct((nd, b, l, d), F32),
        scratch_shapes=[
            pltpu.VMEM((nslab, SSM_BLOCKS * pitch, LANES), F32),
            pltpu.VMEM((nslab, SSM_BLOCKS * pitch, LANES), F32),
            pltpu.VMEM((SSM_BLOCKS, nsb), F32),
            pltpu.VMEM((SSM_BLOCKS, nsb), F32),
            pltpu.VMEM((t, d), BF16),
            pltpu.VMEM((t, 1), F32),
        ],
        compiler_params=_cparams(("arbitrary", "arbitrary", "arbitrary")),
        name="s5_scan",
    )(x, norm_g.reshape(1, d), mod, mod, a_re, a_im, we_re, we_im, wc_re, wc_im)


def _s5_glu_kernel(yf_ref, yb_ref, xf_ref, g_ref, sh_ref, sc_ref, dsk_ref, wa_ref, wg_ref, ba_ref, bg_ref,
                   x_ref, g1_ref, o_ref, v_scr, inv_scr):
    @pl.when(pl.program_id(2) == 0)
    def _():
        def emit(rows, h):
            y = yf_ref[0, 0, rows, :] + yb_ref[0, 0, rows, :] + dsk_ref[...] * h
            v_scr[rows, :] = _gelu_tanh(y).astype(BF16)

        _modnorm_rows(xf_ref, g_ref, sh_ref, sc_ref, inv_scr, emit)

    v = v_scr[...]
    a = jnp.dot(v, wa_ref[...], preferred_element_type=F32) + ba_ref[...]
    g = jnp.dot(v, wg_ref[...], preferred_element_type=F32) + bg_ref[...]
    o_ref[0] = x_ref[0] + g1_ref[0] * (a * jax.nn.sigmoid(g))


def _s5_glu(y2, x, norm_g, mod, d_skip, w_glu_bf, b_glu, tm, tn):
    b, l, d = x.shape
    nj = d // tn
    return pl.pallas_call(
        _s5_glu_kernel,
        grid=(b, l // tm, nj),
        in_specs=[
            pl.BlockSpec((1, 1, tm, d), lambda bi, i, j: (0, bi, i, 0)),
            pl.BlockSpec((1, 1, tm, d), lambda bi, i, j: (1, bi, i, 0)),
            pl.BlockSpec((1, tm, d), lambda bi, i, j: (bi, i, 0)),
            pl.BlockSpec((1, d), lambda bi, i, j: (0, 0)),
            pl.BlockSpec((1, 1, d), lambda bi, i, j: (bi, 0, 0)),
            pl.BlockSpec((1, 1, d), lambda bi, i, j: (bi, 0, 1)),
            pl.BlockSpec((1, d), lambda bi, i, j: (0, 0)),
            pl.BlockSpec((d, tn), lambda bi, i, j: (0, j)),
            pl.BlockSpec((d, tn), lambda bi, i, j: (0, j + nj)),
            pl.BlockSpec((1, tn), lambda bi, i, j: (0, j)),
            pl.BlockSpec((1, tn), lambda bi, i, j: (0, j + nj)),
            pl.BlockSpec((1, tm, tn), lambda bi, i, j: (bi, i, j)),
            pl.BlockSpec((1, 1, tn), lambda bi, i, j: (bi, 0, 2 * nj + j)),
        ],
        out_specs=pl.BlockSpec((1, tm, tn), lambda bi, i, j: (bi, i, j)),
        out_shape=jax.ShapeDtypeStruct((b, l, d), F32),
        scratch_shapes=[pltpu.VMEM((tm, d), BF16), pltpu.VMEM((tm, 1), F32)],
        compiler_params=_cparams(("parallel", "parallel", "arbitrary")),
        name="s5_gelu_glu",
    )(y2, y2, x, norm_g.reshape(1, d), mod, mod, d_skip.reshape(1, d), w_glu_bf, w_glu_bf,
      b_glu.reshape(1, 2 * d), b_glu.reshape(1, 2 * d), x, mod)


def _router_kernel(x_ref, g_ref, sh_ref, sc_ref, wr_ref, hp_ref, aff_ref, hhi_scr, hlo_scr, inv_scr, *, dh):
    def emit(rows, h):
        hb = h.astype(BF16)
        hbf = hb.astype(F32)
        hhi_scr[rows, :] = hb
        hlo_scr[rows, :] = (h - hbf).astype(BF16)
        bits = lax.bitcast_convert_type(hbf, I32)
        lo = lax.shift_right_logical(bits[:, :dh], 16)
        hi = jnp.bitwise_and(bits[:, dh:], jnp.int32(-65536))
        hp_ref[rows, :] = jnp.bitwise_or(hi, lo)

    _modnorm_rows(x_ref, g_ref, sh_ref, sc_ref, inv_scr, emit)

    w = wr_ref[...]
    w_hi = w.astype(BF16)
    w_lo = (w - w_hi.astype(F32)).astype(BF16)
    nt = (((1,), (1,)), ((), ()))
    h_hi = hhi_scr[...]
    logits = (lax.dot_general(w_hi, h_hi, nt, preferred_element_type=F32)
              + lax.dot_general(w_lo, h_hi, nt, preferred_element_type=F32)
              + lax.dot_general(w_hi, hlo_scr[...], nt, preferred_element_type=F32))
    m = jnp.max(logits, axis=0, keepdims=True)
    ex = jnp.exp(logits - m)
    aff_ref[...] = ex / jnp.sum(ex, axis=0, keepdims=True)


def _router(x, norm_g, mod, w_router, tm):
    b, l, d = x.shape
    e = w_router.shape[1]
    dh = d // 2
    nt = l // tm
    kern = functools.partial(_router_kernel, dh=dh)
    return pl.pallas_call(
        kern,
        grid=(b, nt),
        in_specs=[
            pl.BlockSpec((1, tm, d), lambda bi, i: (bi, i, 0)),
            pl.BlockSpec((1, d), lambda bi, i: (0, 0)),
            pl.BlockSpec((1, 1, d), lambda bi, i: (bi, 0, 3)),
            pl.BlockSpec((1, 1, d), lambda bi, i: (bi, 0, 4)),
            pl.BlockSpec((e, d), lambda bi, i: (0, 0)),
        ],
        out_specs=[
            pl.BlockSpec((tm, dh), lambda bi, i: (bi * nt + i, 0)),
            pl.BlockSpec((e, tm), lambda bi, i: (0, bi * nt + i)),
        ],
        out_shape=[jax.ShapeDtypeStruct((b * l, dh), I32), jax.ShapeDtypeStruct((e, b * l), F32)],
        scratch_shapes=[pltpu.VMEM((tm, d), BF16), pltpu.VMEM((tm, d), BF16), pltpu.VMEM((tm, 1), F32)],
        compiler_params=_cparams(("parallel", "parallel")),
        name="moe_router",
    )(x, norm_g.reshape(1, d), mod, mod, w_router.T)


def _excl_cumsum_lanes(x):
    n = x.shape[1]
    lane = lax.broadcasted_iota(I32, x.shape, 1)
    inc = x
    k = 1
    while k < n:
        inc = inc + jnp.where(lane >= k, pltpu.roll(inc, k, axis=1), 0)
        k *= 2
    return inc - x


def _select_kernel(aff_ref, idx_ref, gate_ref, off_ref, pos_scr, vals_scr, racc, off_sm, sem, *, cap, ntile, tmtok):
    e, n = aff_ref.shape
    key = lax.bitcast_convert_type(aff_ref[...], I32)

    def bit_step(it, prefix):
        cand = jnp.bitwise_or(prefix, jnp.left_shift(jnp.int32(1), 30 - it))
        cnt = jnp.sum((key >= cand).astype(I32), axis=1, keepdims=True)
        return jnp.where(cnt >= cap, cand, prefix)

    thr = lax.fori_loop(0, 31, bit_step, jnp.zeros((e, 1), I32))
    gt = key > thr
    eq = key == thr
    need = cap - jnp.sum(gt.astype(I32), axis=1, keepdims=True)
    sel = jnp.logical_or(gt, jnp.logical_and(eq, _excl_cumsum_lanes(eq.astype(I32)) < need))
    sel_i = sel.astype(I32)
    pos_scr[...] = jnp.where(sel, _excl_cumsum_lanes(sel_i), -1)

    tok = lax.broadcasted_iota(I32, (e, n), 1)
    lane = lax.broadcasted_iota(I32, off_ref.shape, 1)
    off = jnp.zeros(off_ref.shape, I32)
    for b in range(1, ntile + 1):
        cnt = jnp.sum(jnp.where(tok < b * tmtok, sel_i, 0), axis=1, keepdims=True)
        off = jnp.where(lane == b, cnt, off)
    off_ref[...] = off
    off_copy = pltpu.make_async_copy(off_ref, off_sm, sem)
    off_copy.start()

    tok1 = lax.broadcasted_iota(I32, vals_scr.shape, 1)
    row = lax.broadcasted_iota(I32, vals_scr.shape, 0)
    tok_hi = jnp.right_shift(tok1, 7).astype(F32)
    tok_lo = jnp.bitwise_and(tok1, 127).astype(F32)
    off_copy.wait()

    def expert(ei, _):
        g = aff_ref[pl.ds(ei, 1), :]
        g0 = g.astype(BF16).astype(F32)
        r1 = g - g0
        g1 = r1.astype(BF16).astype(F32)
        g2 = r1 - g1
        vals_scr[...] = jnp.where(row == 0, tok_hi, jnp.where(row == 1, tok_lo, jnp.where(
            row == 2, g0, jnp.where(row == 3, g1, jnp.where(row == 4, g2, 0.0))))).astype(BF16)

        def chunk(ci, _):
            s0 = ci * SLOT_BLOCK
            slot = s0 + lax.broadcasted_iota(I32, (SLOT_BLOCK, tmtok), 0)
            racc[...] = jnp.zeros_like(racc)
            for tb in range(ntile):
                @pl.when(jnp.logical_and(off_sm[ei, tb] < s0 + SLOT_BLOCK, off_sm[ei, tb + 1] > s0))
                def _():
                    cols = pl.ds(tb * tmtok, tmtok)
                    onehot = jnp.where(pos_scr[pl.ds(ei, 1), cols] == slot, 1.0, 0.0).astype(BF16)
                    racc[...] += lax.dot_general(vals_scr[:, cols], onehot, (((1,), (1,)), ((), ())),
                                                 preferred_element_type=F32)
            r = racc[...]
            idx_ref[ei, pl.ds(ci, 1), :] = (r[0:1] * 128.0 + r[1:2]).astype(I32)
            gate_ref[ei, pl.ds(ci, 1), :] = r[2:3] + r[3:4] + r[4:5]
            return 0

        lax.fori_loop(0, cap // SLOT_BLOCK, chunk, 0)
        return 0

    lax.fori_loop(0, e, expert, 0)


def _select(aff_t, cap, ntile, tmtok):
    e, n = aff_t.shape
    nc = cap // SLOT_BLOCK
    kern = functools.partial(_select_kernel, cap=cap, ntile=ntile, tmtok=tmtok)
    idx, gate, off = pl.pallas_call(
        kern,
        out_shape=[jax.ShapeDtypeStruct((e, nc, SLOT_BLOCK), I32),
                   jax.ShapeDtypeStruct((e, nc, SLOT_BLOCK), F32),
                   jax.ShapeDtypeStruct((e, LANES), I32)],
        scratch_shapes=[pltpu.VMEM((e, n), I32), pltpu.VMEM((16, n), BF16), pltpu.VMEM((16, SLOT_BLOCK), F32),
                        pltpu.SMEM((e, LANES), I32), pltpu.SemaphoreType.DMA(())],
        compiler_params=pltpu.CompilerParams(vmem_limit_bytes=VMEM_LIMIT),
        name="moe_select",
    )(aff_t)
    return idx.reshape(e, cap), gate.reshape(e, cap), off


def _ffn_kernel(idx_sm, *refs, ngrp, cap, nf, tf, dh, rows_per_step):
    hp_hbm = refs[:ngrp]
    wg_ref, wu_ref, wd_ref, o_ref, xp_scr, xs_scr, hid_scr, sem = refs[ngrp:]
    e = pl.program_id(0)
    s = pl.program_id(1)
    ne = pl.num_programs(0)
    m = ngrp * cap

    def issue_gather(expert):
        for grp in range(ngrp):
            base = (grp * ne + expert) * cap

            def body(c, _):
                pltpu.make_async_copy(hp_hbm[grp].at[pl.ds(idx_sm[base + c], 1), :],
                                      xp_scr.at[pl.ds(grp * cap + c, 1), :], sem).start()
                return 0

            lax.fori_loop(0, cap, body, 0, unroll=8)

    def wait_gather():
        pltpu.make_async_copy(hp_hbm[0].at[pl.ds(0, m), :], xp_scr, sem).wait()

    nstep = pl.num_programs(1)

    @pl.when(jnp.logical_and(e == 0, s == 0))
    def _():
        issue_gather(0)

    @pl.when(s == 0)
    def _():
        wait_gather()
        p = xp_scr[...]
        xs_scr[0] = lax.bitcast_convert_type(jnp.left_shift(p, 16), F32).astype(BF16)
        xs_scr[1] = lax.bitcast_convert_type(jnp.bitwise_and(p, jnp.int32(-65536)), F32).astype(BF16)

    nxt = jnp.minimum(e + 1, ne - 1)

    def issue_slice():
        for grp in range(ngrp):
            base = (grp * ne + nxt) * cap + s * rows_per_step
            for c in range(rows_per_step):
                pltpu.make_async_copy(hp_hbm[grp].at[pl.ds(idx_sm[base + c], 1), :],
                                      xp_scr.at[pl.ds(grp * cap + s * rows_per_step + c, 1), :], sem).start()

    @pl.when(s < nf)
    def _():
        issue_slice()
        wg = wg_ref[0, 0].astype(BF16)
        wu = wu_ref[0, 0].astype(BF16)
        xl = xs_scr[0]
        xh = xs_scr[1]
        g = (jnp.dot(xl, wg[:dh], preferred_element_type=F32) + jnp.dot(xh, wg[dh:], preferred_element_type=F32))
        u = (jnp.dot(xl, wu[:dh], preferred_element_type=F32) + jnp.dot(xh, wu[dh:], preferred_element_type=F32))
        hid_scr[s] = (_silu(g) * u).astype(BF16)

    @pl.when(s >= nf)
    def _():
        issue_slice()
        wd = wd_ref[0, 0].astype(BF16)
        acc = jnp.dot(hid_scr[0], wd[:tf], preferred_element_type=F32)
        for k in range(1, nf):
            acc = acc + jnp.dot(hid_scr[k], wd[k * tf:(k + 1) * tf], preferred_element_type=F32)
        for grp in range(ngrp):
            o_ref[grp, 0] = acc[grp * cap:(grp + 1) * cap]

    @pl.when(jnp.logical_and(e == ne - 1, s == nstep - 1))
    def _():
        wait_gather()


def _ffn(idx_flat, hps, w_gate, w_up, w_down, layer, cap, tf, tn):
    ngrp = len(hps)
    n, dh = hps[0].shape
    _, ne, d, f = w_gate.shape
    nf = f // tf
    nn = d // tn
    m = ngrp * cap
    assert all(hp.shape[0] >= m for hp in hps)
    assert cap % (nf + nn) == 0
    kern = functools.partial(_ffn_kernel, ngrp=ngrp, cap=cap, nf=nf, tf=tf, dh=dh, rows_per_step=cap // (nf + nn))
    grid_spec = pltpu.PrefetchScalarGridSpec(
        num_scalar_prefetch=1,
        grid=(ne, nf + nn),
        in_specs=[pl.BlockSpec(memory_space=pl.ANY)] * ngrp + [
            pl.BlockSpec((1, 1, d, tf), lambda e, s, idx: (layer, e, 0, jnp.minimum(s, nf - 1))),
            pl.BlockSpec((1, 1, d, tf), lambda e, s, idx: (layer, e, 0, jnp.minimum(s, nf - 1))),
            pl.BlockSpec((1, 1, f, tn), lambda e, s, idx: (layer, e, 0, jnp.maximum(s - nf, 0))),
        ],
        out_specs=pl.BlockSpec((ngrp, 1, cap, tn), lambda e, s, idx: (0, e, 0, jnp.maximum(s - nf, 0))),
        scratch_shapes=[
            pltpu.VMEM((m, dh), I32),
            pltpu.VMEM((2, m, dh), BF16),
            pltpu.VMEM((nf, m, tf), BF16),
            pltpu.SemaphoreType.DMA(()),
        ],
    )
    return pl.pallas_call(
        kern,
        grid_spec=grid_spec,
        out_shape=jax.ShapeDtypeStruct((ngrp, ne, cap, d), F32),
        compiler_params=_cparams(("arbitrary", "arbitrary")),
        name="moe_ffn",
    )(idx_flat, *hps, w_gate, w_up, w_down)


def _worklist(off, ne, ntile, nblk):
    bs = SLOT_BLOCK
    w_max = ne * (nblk + ntile - 1) + ntile
    lo = off[:, :ntile].T.reshape(-1)
    hi = off[:, 1:ntile + 1].T.reshape(-1)
    b0 = jnp.minimum(lo // bs, nblk - 1)
    nb = jnp.where(hi > lo, (hi + bs - 1) // bs - b0, 0)
    first_e = (jnp.arange(ntile * ne, dtype=I32) % ne) == 0
    nb = jnp.where(first_e, jnp.maximum(nb, 1), nb)
    end = jnp.cumsum(nb)
    total = end[-1]
    w = jnp.arange(w_max, dtype=I32)
    wc = jnp.minimum(w, total - 1)
    pair = jnp.sum((end[None, :] <= wc[:, None]).astype(I32), axis=1)
    table = jnp.stack([end - nb, b0, lo, hi], axis=1)
    onehot = pair[:, None] == jnp.arange(ntile * ne, dtype=I32)[None, :]
    got = jnp.sum(jnp.where(onehot[:, :, None], table[None], 0), axis=1)
    tile = pair // ne
    exp = pair % ne
    blk = got[:, 1] + (wc - got[:, 0])
    valid = w < total
    rlo = jnp.where(valid, jnp.clip(got[:, 2] - blk * bs, 0, bs), 0)
    rhi = jnp.where(valid, jnp.clip(got[:, 3] - blk * bs, 0, bs), 0)
    first = jnp.concatenate([jnp.ones((1,), bool), tile[1:] != tile[:-1]])
    last = jnp.concatenate([tile[1:] != tile[:-1], jnp.ones((1,), bool)])
    to_i = lambda a: a.astype(I32)
    return to_i(tile), to_i(exp), to_i(blk), to_i(rlo), to_i(rhi), to_i(first), to_i(last)


def _combine_kernel(tile_sm, exp_sm, blk_sm, rlo_sm, rhi_sm, first_sm, last_sm, idx_sm, gate_sm,
                    src_ref, x_ref, g2_ref, fg_ref, o_ref, acc_scr, *, cap, tmtok, final):
    w = pl.program_id(0)

    @pl.when(first_sm[w] == 1)
    def _():
        acc_scr[...] = jnp.zeros_like(acc_scr)

    slot0 = exp_sm[w] * cap + blk_sm[w] * SLOT_BLOCK
    tok0 = tile_sm[w] * tmtok

    rlo = rlo_sm[w]
    rhi = rhi_sm[w]
    body0 = jnp.minimum(((rlo + COMBINE_ROWS - 1) // COMBINE_ROWS) * COMBINE_ROWS, rhi)
    body1 = jnp.maximum((rhi // COMBINE_ROWS) * COMBINE_ROWS, body0)

    def row(r, _):
        tok = idx_sm[slot0 + r] - tok0
        acc_scr[pl.ds(tok, 1), :] = acc_scr[pl.ds(tok, 1), :] + src_ref[0, 0, pl.ds(r, 1), :] * gate_sm[slot0 + r]
        return 0

    def group(gi, _):
        r0 = pl.multiple_of(body0 + gi * COMBINE_ROWS, COMBINE_ROWS)
        toks = [idx_sm[slot0 + r0 + u] - tok0 for u in range(COMBINE_ROWS)]
        new = [acc_scr[pl.ds(toks[u], 1), :] + src_ref[0, 0, pl.ds(r0 + u, 1), :] * gate_sm[slot0 + r0 + u]
               for u in range(COMBINE_ROWS)]
        for u in range(COMBINE_ROWS):
            acc_scr[pl.ds(toks[u], 1), :] = new[u]
        return 0

    lax.fori_loop(rlo, body0, row, 0)
    lax.fori_loop(0, (body1 - body0) // COMBINE_ROWS, group, 0)
    lax.fori_loop(body1, rhi, row, 0)

    @pl.when(last_sm[w] == 1)
    def _():
        xn = x_ref[0] + g2_ref[0] * acc_scr[...]
        if final:
            ms = jnp.mean(xn * xn, axis=-1, keepdims=True)
            xn = xn * lax.rsqrt(ms + RMS_EPS) * fg_ref[...]
        o_ref[0] = xn


def _combine(work, idx_flat_g, gate_flat_g, outs, grp, x, mod, final_g, tmtok, final):
    b, l, d = x.shape
    ne, cap = outs.shape[1], outs.shape[2]
    tpb = l // tmtok
    w_max = work[0].shape[0]
    kern = functools.partial(_combine_kernel, cap=cap, tmtok=tmtok, final=final)
    nj = 1
    grid_spec = pltpu.PrefetchScalarGridSpec(
        num_scalar_prefetch=9,
        grid=(w_max,),
        in_specs=[
            pl.BlockSpec((1, 1, SLOT_BLOCK, d),
                         lambda w, ti, ex, bl, *_: (grp, ex[w], bl[w], 0)),
            pl.BlockSpec((1, tmtok, d), lambda w, ti, *_: (ti[w] // tpb, ti[w] % tpb, 0),
                         pipeline_mode=pl.Buffered(1)),
            pl.BlockSpec((1, 1, d), lambda w, ti, *_: (ti[w] // tpb, 0, 5 * nj)),
            pl.BlockSpec((1, d), lambda w, *_: (0, 0)),
        ],
        out_specs=pl.BlockSpec((1, tmtok, d), lambda w, ti, *_: (ti[w] // tpb, ti[w] % tpb, 0)),
        scratch_shapes=[pltpu.VMEM((tmtok, d), F32)],
    )
    return pl.pallas_call(
        kern,
        grid_spec=grid_spec,
        out_shape=jax.ShapeDtypeStruct((b, l, d), F32),
        compiler_params=_cparams(("arbitrary",)),
        name="moe_combine",
    )(*work, idx_flat_g, gate_flat_g, outs, x, mod, final_g.reshape(1, d))


def _pick(n, pref):
    t = min(pref, n)
    while n % t:
        t //= 2
    return t


def _moe_layer(xs, mods, norm_g, w_router, w_gate, w_up, w_down, layer, final_g, final):
    ne = w_router.shape[1]
    hps, idxs, gates, works = [], [], [], []
    caps = set()
    for x, mod in zip(xs, mods):
        b, l, d = x.shape
        n = b * l
        cap = EC_CAPACITY_FACTOR * n // ne
        caps.add(cap)
        tmtok = _pick(l, 1024)
        ntile = n // tmtok
        hp, aff_t = _router(x, norm_g, mod, w_router, _pick(l, 512))
        idx, gate, off = _select(aff_t, cap, ntile, tmtok)
        hps.append(hp)
        idxs.append(idx.reshape(-1))
        gates.append(gate.reshape(-1))
        works.append(_worklist(off, ne, ntile, cap // SLOT_BLOCK))
    assert len(caps) == 1, "request groups must have equal expert capacity"
    cap = caps.pop()
    f = w_gate.shape[3]
    outs = _ffn(jnp.concatenate(idxs), hps, w_gate, w_up, w_down, layer, cap,
                _pick(f, 256), _pick(xs[0].shape[2], 256))
    new = []
    for grp, (x, mod) in enumerate(zip(xs, mods)):
        tmtok = _pick(x.shape[1], 1024)
        new.append(_combine(works[grp], idxs[grp], gates[grp], outs, grp, x, mod, final_g, tmtok, final))
    return new


def kernel(x_prompt, x_sample, c_prompt, c_sample, ada_w, ada_b, norm_mix_g, norm_ffn_g, final_norm_g, conv_w_in, conv_b_in, conv_w_dw, conv_b_dw, conv_ln_g, conv_ln_b, conv_w_out, conv_b_out, ssm_lambda_re, ssm_lambda_im, ssm_log_step, ssm_b_re, ssm_b_im, ssm_c_re, ssm_c_im, ssm_d, ssm_w_glu, ssm_b_glu, moe_w_router, moe_w_gate, moe_w_up, moe_w_down):
    depth = ada_w.shape[0]
    d = x_prompt.shape[-1]
    bp, bs = x_prompt.shape[0], x_sample.shape[0]
    nrow = 16 * ((bp + bs + 15) // 16)
    c_all = jnp.concatenate([c_prompt, c_sample, jnp.zeros((nrow - bp - bs, d), F32)], axis=0)
    mod_all = _ada(c_all, ada_w, ada_b)

    xs = [x_prompt, x_sample]
    for i in range(depth):
        mods = [mod_all[i, :bp].reshape(bp, 1, 6 * d), mod_all[i, bp:bp + bs].reshape(bs, 1, 6 * d)]
        j = i // 2
        if i % 2 == 0:
            w_in = conv_w_in[j].astype(BF16)
            w_out = conv_w_out[j].astype(BF16)
            nxt = []
            for x, mod in zip(xs, mods):
                l = x.shape[1]
                u = _conv_in(x, norm_mix_g[i], mod, w_in, conv_b_in[j], _pick(l, 512), _pick(d, 1024))
                nxt.append(_conv_out(u, x, mod, conv_w_dw[j], conv_b_dw[j], conv_ln_g[j], conv_ln_b[j],
                                     w_out, conv_b_out[j], _pick(l, 512), _pick(d, 1024)))
            xs = nxt
        else:
            lay = _s5_layout(ssm_lambda_re[j], ssm_lambda_im[j], ssm_log_step[j], ssm_b_re[j], ssm_b_im[j],
                             ssm_c_re[j], ssm_c_im[j])
            w_glu = ssm_w_glu[j].astype(BF16)
            nxt = []
            for x, mod in zip(xs, mods):
                l = x.shape[1]
                y2 = _s5_scan(x, norm_mix_g[i], mod, lay, _pick(l, 256))
                nxt.append(_s5_glu(y2, x, norm_mix_g[i], mod, ssm_d[j], w_glu, ssm_b_glu[j],
                                   _pick(l, 512), _pick(d, 1024)))
            xs = nxt
        xs = _moe_layer(xs, mods, norm_ffn_g[i], moe_w_router[i], moe_w_gate, moe_w_up, moe_w_down, i,
                        final_norm_g, final=(i == depth - 1))
    return (xs[0], xs[1])
```

```python
import functools
import math

import jax
import jax.numpy as jnp
from jax import lax
from jax.experimental import pallas as pl
from jax.experimental.pallas import tpu as pltpu

F32 = jnp.float32
BF16 = jnp.bfloat16
I32 = jnp.int32
HIGHEST = lax.Precision.HIGHEST

RMS_EPS = 1e-6
LN_EPS = 1e-5
CONV_WIDTH = 31
CONV_PAD = (CONV_WIDTH - 1) // 2
CONV_HALO = 16
SSM_GROUP = 16
SSM_STATE = 64
SSM_BLOCKS = 8
S5_LAG = 4
N_EXPERTS = 16
EC_CAPACITY_FACTOR = 2
LANES = 128
SLOT_BLOCK = 128
COMBINE_ROWS = 8
ROW_CHUNK = 16

VMEM_LIMIT = 56 * 1024 * 1024


def _cparams(sem, vmem=VMEM_LIMIT):
    return pltpu.CompilerParams(dimension_semantics=sem, vmem_limit_bytes=vmem)


def _modnorm_rows(x_ref, g_ref, sh_ref, sc_ref, inv_scr, emit):
    x = x_ref[0]
    inv_scr[...] = lax.rsqrt(jnp.mean(x * x, axis=-1, keepdims=True) + RMS_EPS)
    gain = g_ref[...] * (1.0 + sc_ref[0])
    shift = sh_ref[0]

    def chunk(r, _):
        rows = pl.ds(pl.multiple_of(r * ROW_CHUNK, ROW_CHUNK), ROW_CHUNK)
        emit(rows, x_ref[0, rows, :] * inv_scr[rows, :] * gain + shift)
        return 0

    lax.fori_loop(0, x.shape[0] // ROW_CHUNK, chunk, 0, unroll=2)


def _silu(x):
    return x * jax.nn.sigmoid(x)


def _gelu_tanh(x):
    c = math.sqrt(2.0 / math.pi)
    return 0.5 * x * (1.0 + jnp.tanh(c * (x + 0.044715 * (x * x * x))))


def _ada_kernel(c_ref, w_ref, b_ref, o_ref):
    c = c_ref[...]
    cond = _silu(c)
    c_hi = cond.astype(BF16)
    c_lo = (cond - c_hi.astype(F32)).astype(BF16)
    w = w_ref[0]
    w_hi = w.astype(BF16)
    w_lo = (w - w_hi.astype(F32)).astype(BF16)
    o_ref[0] = (jnp.dot(c_hi, w_hi, preferred_element_type=F32) + jnp.dot(c_lo, w_hi, preferred_element_type=F32)
                + jnp.dot(c_hi, w_lo, preferred_element_type=F32) + b_ref[0])


def _ada(c_all, ada_w, ada_b):
    depth, d, n6 = ada_w.shape
    rows = c_all.shape[0]
    tn = _pick(n6, 1024)
    return pl.pallas_call(
        _ada_kernel,
        grid=(depth, n6 // tn),
        in_specs=[
            pl.BlockSpec((rows, d), lambda i, j: (0, 0)),
            pl.BlockSpec((1, d, tn), lambda i, j: (i, 0, j)),
            pl.BlockSpec((1, 1, tn), lambda i, j: (i, 0, j)),
        ],
        out_specs=pl.BlockSpec((1, rows, tn), lambda i, j: (i, 0, j)),
        out_shape=jax.ShapeDtypeStruct((depth, rows, n6), F32),
        compiler_params=_cparams(("parallel", "parallel")),
        name="ada_mod",
    )(c_all, ada_w, ada_b.reshape(depth, 1, n6))


def _conv_in_kernel(x_ref, g_ref, sh_ref, sc_ref, wa_ref, wg_ref, ba_ref, bg_ref, o_ref, h_scr, inv_scr):
    @pl.when(pl.program_id(2) == 0)
    def _():
        def emit(rows, h):
            h_scr[rows, :] = h.astype(BF16)

        _modnorm_rows(x_ref, g_ref, sh_ref, sc_ref, inv_scr, emit)

    h = h_scr[...]
    a = jnp.dot(h, wa_ref[...], preferred_element_type=F32) + ba_ref[...]
    g = jnp.dot(h, wg_ref[...], preferred_element_type=F32) + bg_ref[...]
    o_ref[0] = a * jax.nn.sigmoid(g)


def _conv_in(x, norm_g, mod, w_in_bf, b_in, tm, tn):
    b, l, d = x.shape
    nj = d // tn
    return pl.pallas_call(
        _conv_in_kernel,
        grid=(b, l // tm, nj),
        in_specs=[
            pl.BlockSpec((1, tm, d), lambda bi, i, j: (bi, i, 0)),
            pl.BlockSpec((1, d), lambda bi, i, j: (0, 0)),
            pl.BlockSpec((1, 1, d), lambda bi, i, j: (bi, 0, 0)),
            pl.BlockSpec((1, 1, d), lambda bi, i, j: (bi, 0, 1)),
            pl.BlockSpec((d, tn), lambda bi, i, j: (0, j)),
            pl.BlockSpec((d, tn), lambda bi, i, j: (0, j + nj)),
            pl.BlockSpec((1, tn), lambda bi, i, j: (0, j)),
            pl.BlockSpec((1, tn), lambda bi, i, j: (0, j + nj)),
        ],
        out_specs=pl.BlockSpec((1, tm, tn), lambda bi, i, j: (bi, i, j)),
        out_shape=jax.ShapeDtypeStruct((b, l, d), F32),
        scratch_shapes=[pltpu.VMEM((tm, d), BF16), pltpu.VMEM((tm, 1), F32)],
        compiler_params=_cparams(("parallel", "parallel", "arbitrary")),
        name="conv_in_glu",
    )(x, norm_g.reshape(1, d), mod, mod, w_in_bf, w_in_bf, b_in.reshape(1, 2 * d), b_in.reshape(1, 2 * d))


def _conv_out_kernel(u_ref, up_ref, un_ref, wdw_ref, bdw_ref, lg_ref, lb_ref, wo_ref, bo_ref, x_ref, g1_ref,
                     o_ref, ext_scr, c_scr, v_scr, mu_scr, inv_scr, *, tm, d, rc, cc):
    i = pl.program_id(1)
    ni = pl.num_programs(1)

    ncol = d // cc

    @pl.when(pl.program_id(2) == 0)
    def _():
        for c in range(ncol):
            cols = pl.ds(c * cc, cc)
            ext_scr[c, pl.ds(0, CONV_HALO), :] = jnp.where(i > 0, up_ref[0, :, cols], 0.0)
            ext_scr[c, pl.ds(CONV_HALO, tm), :] = u_ref[0, :, cols]
            ext_scr[c, pl.ds(CONV_HALO + tm, CONV_HALO), :] = jnp.where(i < ni - 1, un_ref[0, :, cols], 0.0)

        def col_chunk(c, _):
            for r0 in range(0, tm, rc):
                acc = jnp.zeros((rc, cc), F32) + bdw_ref[c]
                for k in range(CONV_WIDTH):
                    seg = ext_scr.at[c][pl.ds(r0 + (CONV_HALO - CONV_PAD + k), rc, stride=1), :]
                    acc = acc + seg * wdw_ref[c, pl.ds(k, 1), :]
                c_scr[c, pl.ds(r0, rc), :] = acc
            return 0

        lax.fori_loop(0, ncol, col_chunk, 0)

        def rows_of(r):
            return pl.ds(pl.multiple_of(r * ROW_CHUNK, ROW_CHUNK), ROW_CHUNK)

        def mean_chunk(r, _):
            rows = rows_of(r)
            tot = functools.reduce(lambda a, v: a + v, [c_scr[c, rows, :] for c in range(ncol)])
            mu_scr[rows, :] = jnp.sum(tot, axis=-1, keepdims=True) * (1.0 / d)
            return 0

        def var_chunk(r, _):
            rows = rows_of(r)
            mu = mu_scr[rows, :]
            dev = [c_scr[c, rows, :] - mu for c in range(ncol)]
            tot = functools.reduce(lambda a, v: a + v, [v * v for v in dev])
            inv_scr[rows, :] = lax.rsqrt(jnp.sum(tot, axis=-1, keepdims=True) * (1.0 / d) + LN_EPS)
            return 0

        def norm_chunk(r, _):
            rows = rows_of(r)
            mu = mu_scr[rows, :]
            inv = inv_scr[rows, :]
            for c in range(ncol):
                cols = pl.ds(c * cc, cc)
                y = (c_scr[c, rows, :] - mu) * inv * lg_ref[:, cols] + lb_ref[:, cols]
                v_scr[rows, cols] = _silu(y).astype(BF16)
            return 0

        for body, unroll in ((mean_chunk, 16), (var_chunk, 4), (norm_chunk, 4)):
            lax.fori_loop(0, tm // ROW_CHUNK, body, 0, unroll=min(unroll, tm // ROW_CHUNK))

    o = jnp.dot(v_scr[...], wo_ref[...], preferred_element_type=F32) + bo_ref[...]
    o_ref[0] = x_ref[0] + g1_ref[0] * o


def _conv_out(u, x, mod, w_dw, b_dw, ln_g, ln_b, w_out_bf, b_out, tm, tn):
    b, l, d = x.shape
    nj = d // tn
    hb = tm // CONV_HALO
    nhb = l // CONV_HALO
    rc = min(64, tm)
    cc = min(LANES, d)
    ncol = d // cc
    w_dw_p = jnp.concatenate([w_dw, jnp.zeros((1, d), w_dw.dtype)], axis=0)
    w_dw_p = w_dw_p.reshape(CONV_WIDTH + 1, ncol, cc).transpose(1, 0, 2)
    kern = functools.partial(_conv_out_kernel, tm=tm, d=d, rc=rc, cc=cc)
    return pl.pallas_call(
        kern,
        grid=(b, l // tm, nj),
        in_specs=[
            pl.BlockSpec((1, tm, d), lambda bi, i, j: (bi, i, 0)),
            pl.BlockSpec((1, CONV_HALO, d), lambda bi, i, j: (bi, jnp.maximum(i * hb - 1, 0), 0)),
            pl.BlockSpec((1, CONV_HALO, d), lambda bi, i, j: (bi, jnp.minimum((i + 1) * hb, nhb - 1), 0)),
            pl.BlockSpec((ncol, CONV_WIDTH + 1, cc), lambda bi, i, j: (0, 0, 0)),
            pl.BlockSpec((ncol, 1, cc), lambda bi, i, j: (0, 0, 0)),
            pl.BlockSpec((1, d), lambda bi, i, j: (0, 0)),
            pl.BlockSpec((1, d), lambda bi, i, j: (0, 0)),
            pl.BlockSpec((d, tn), lambda bi, i, j: (0, j)),
            pl.BlockSpec((1, tn), lambda bi, i, j: (0, j)),
            pl.BlockSpec((1, tm, tn), lambda bi, i, j: (bi, i, j)),
            pl.BlockSpec((1, 1, tn), lambda bi, i, j: (bi, 0, 2 * nj + j)),
        ],
        out_specs=pl.BlockSpec((1, tm, tn), lambda bi, i, j: (bi, i, j)),
        out_shape=jax.ShapeDtypeStruct((b, l, d), F32),
        scratch_shapes=[
            pltpu.VMEM((ncol, tm + 2 * CONV_HALO, cc), F32),
            pltpu.VMEM((ncol, tm, cc), F32),
            pltpu.VMEM((tm, d), BF16),
            pltpu.VMEM((tm, 1), F32),
            pltpu.VMEM((tm, 1), F32),
        ],
        compiler_params=_cparams(("parallel", "parallel", "arbitrary")),
        name="conv_dw_ln_out",
    )(u, u, u, w_dw_p, b_dw.reshape(ncol, 1, cc), ln_g.reshape(1, d), ln_b.reshape(1, d), w_out_bf,
      b_out.reshape(1, d), x, mod)


def _zoh_discretise(lam_re, lam_im, log_step, b_re, b_im):
    dt = jnp.exp(log_step)[..., None]
    mag = jnp.exp(lam_re * dt)
    a_re = mag * jnp.cos(lam_im * dt)
    a_im = mag * jnp.sin(lam_im * dt)
    nr = a_re - 1.0
    ni = a_im
    den = lam_re * lam_re + lam_im * lam_im
    k_re = (nr * lam_re + ni * lam_im) / den
    k_im = (ni * lam_re - nr * lam_im) / den
    bb_re = k_re[..., None] * b_re - k_im[..., None] * b_im
    bb_im = k_re[..., None] * b_im + k_im[..., None] * b_re
    return a_re, a_im, bb_re, bb_im


def _s5_layout(lam_re, lam_im, log_step, b_re, b_im, c_re, c_im):
    a_re, a_im, bb_re, bb_im = _zoh_discretise(lam_re, lam_im, log_step, b_re, b_im)
    nd, g, p = a_re.shape
    c = b_re.shape[-1]
    gpb = g // SSM_BLOCKS
    chb, nsb = gpb * c, gpb * p

    def rows_gc(bb):
        return bb.reshape(nd, SSM_BLOCKS, gpb, p, c).transpose(0, 1, 2, 4, 3).reshape(nd, SSM_BLOCKS, chb, p)

    def rows_gp(cc):
        return cc.reshape(nd, SSM_BLOCKS, gpb, c, p).transpose(0, 1, 2, 4, 3).reshape(nd, SSM_BLOCKS, nsb, c)

    def kern(er_ref, ei_ref, cr_ref, ci_ref, wer_ref, wei_ref, wcr_ref, wci_ref):
        def spread(x, width, rows_per_group, cols_per_group):
            k = x.shape[1]
            rep = (lax.broadcasted_iota(I32, (k, width), 1) % k == lax.broadcasted_iota(I32, (k, width), 0))
            full = jnp.dot(x.astype(BF16), rep.astype(BF16), preferred_element_type=F32)
            rg = lax.broadcasted_iota(I32, full.shape, 0) // rows_per_group
            cg = lax.broadcasted_iota(I32, full.shape, 1) // cols_per_group
            return jnp.where(rg == cg, full, 0.0).astype(BF16)

        wer_ref[0, 0] = spread(er_ref[0, 0], nsb, c, p)
        wei_ref[0, 0] = spread(ei_ref[0, 0], nsb, c, p)
        wcr_ref[0, 0] = spread(cr_ref[0, 0], chb, p, c)
        wci_ref[0, 0] = spread(ci_ref[0, 0], chb, p, c)

    blk = lambda r, k: pl.BlockSpec((1, 1, r, k), lambda dd, s: (dd, s, 0, 0))
    we_re, we_im, wc_re, wc_im = pl.pallas_call(
        kern,
        grid=(nd, SSM_BLOCKS),
        in_specs=[blk(chb, p), blk(chb, p), blk(nsb, c), blk(nsb, c)],
        out_specs=[blk(chb, nsb), blk(chb, nsb), blk(nsb, chb), blk(nsb, chb)],
        out_shape=[jax.ShapeDtypeStruct((nd, SSM_BLOCKS, chb, nsb), BF16)] * 2
        + [jax.ShapeDtypeStruct((nd, SSM_BLOCKS, nsb, chb), BF16)] * 2,
        compiler_params=_cparams(("parallel", "parallel")),
        name="s5_weight_layout",
    )(rows_gc(bb_re), rows_gc(bb_im), rows_gp(c_re), rows_gp(c_im))
    return (a_re.reshape(nd, SSM_BLOCKS, nsb), a_im.reshape(nd, SSM_BLOCKS, nsb), we_re, we_im, wc_re, wc_im)


def _s5_kernel(x_ref, g_ref, sh_ref, sc_ref, ar_ref, ai_ref, wer_ref, wei_ref, wcr_ref, wci_ref,
               o_ref, sre, sim, cre, cim, h_scr, inv_scr, *, t, chb, nsb, pitch):
    dirn = pl.program_id(0)
    nslab = nsb // LANES

    @pl.when(jnp.logical_and(jnp.logical_and(dirn == 0, pl.program_id(1) == 0), pl.program_id(2) == 0))
    def _():
        sre[...] = jnp.zeros_like(sre)
        sim[...] = jnp.zeros_like(sim)

    @pl.when(pl.program_id(2) == 0)
    def _():
        cre[...] = jnp.zeros_like(cre)
        cim[...] = jnp.zeros_like(cim)

    def emit(rows, h):
        h_scr[rows, :] = h.astype(BF16)

    _modnorm_rows(x_ref, g_ref, sh_ref, sc_ref, inv_scr, emit)

    def base(s):
        return s * pitch + (S5_LAG if s % 2 else 0)

    for s in range(SSM_BLOCKS):
        hs = h_scr[:, s * chb:(s + 1) * chb]
        bur = jnp.dot(hs, wer_ref[0, s], preferred_element_type=F32)
        bui = jnp.dot(hs, wei_ref[0, s], preferred_element_type=F32)
        for j in range(nslab):
            sre[j, pl.ds(base(s), t), :] = bur[:, j * LANES:(j + 1) * LANES]
            sim[j, pl.ds(base(s), t), :] = bui[:, j * LANES:(j + 1) * LANES]

    ar = [ar_ref[0, :, j * LANES:(j + 1) * LANES] for j in range(nslab)]
    ai = [ai_ref[0, :, j * LANES:(j + 1) * LANES] for j in range(nslab)]
    odd = jnp.bitwise_and(lax.broadcasted_iota(I32, (SSM_BLOCKS, LANES), 0), 1)
    nrow = t + S5_LAG

    def make_step(active):
        def step(k, carry):
            xr, xi = carry
            row = jnp.where(dirn == 0, k, nrow - 1 - k)
            rows = pl.ds(row, SSM_BLOCKS, stride=pitch)
            nr, ni = [], []
            for j in range(nslab):
                br = sre.at[j][rows, :]
                bi = sim.at[j][rows, :]
                r = ar[j] * xr[j] - ai[j] * xi[j] + br
                im = ar[j] * xi[j] + ai[j] * xr[j] + bi
                sre.at[j][rows, :] = r
                sim.at[j][rows, :] = im
                if active is not None:
                    r = jnp.where(active, r, xr[j])
                    im = jnp.where(active, im, xi[j])
                nr.append(r)
                ni.append(im)
            return tuple(nr), tuple(ni)
        return step

    x0 = (tuple(cre[:, j * LANES:(j + 1) * LANES] for j in range(nslab)),
          tuple(cim[:, j * LANES:(j + 1) * LANES] for j in range(nslab)))
    x1 = lax.fori_loop(0, S5_LAG, make_step(odd == dirn), x0)
    x2 = lax.fori_loop(S5_LAG, t, make_step(None), x1, unroll=4)
    xr, xi = lax.fori_loop(t, nrow, make_step(odd != dirn), x2)
    for j in range(nslab):
        cre[:, j * LANES:(j + 1) * LANES] = xr[j]
        cim[:, j * LANES:(j + 1) * LANES] = xi[j]

    for s in range(SSM_BLOCKS):
        sr = jnp.concatenate([sre[j, pl.ds(base(s), t), :] for j in range(nslab)], axis=1).astype(BF16)
        si = jnp.concatenate([sim[j, pl.ds(base(s), t), :] for j in range(nslab)], axis=1).astype(BF16)
        y = (jnp.dot(sr, wcr_ref[0, s], preferred_element_type=F32)
             - jnp.dot(si, wci_ref[0, s], preferred_element_type=F32))
        o_ref[0, 0, :, s * chb:(s + 1) * chb] = y


def _s5_scan(x, norm_g, mod, lay, t):
    a_re, a_im, we_re, we_im, wc_re, wc_im = lay
    b, l, d = x.shape
    nd = a_re.shape[0]
    chb = d // SSM_BLOCKS
    nsb = a_re.shape[-1]
    nslab = nsb // LANES
    nt = l // t
    pitch = t + S5_LAG
    kern = functools.partial(_s5_kernel, t=t, chb=chb, nsb=nsb, pitch=pitch)

    def tile(dd, i):
        return i + dd * (nt - 1 - 2 * i)

    single = pl.Buffered(1)
    return pl.pallas_call(
        kern,
        grid=(nd, b, nt),
        in_specs=[
            pl.BlockSpec((1, t, d), lambda dd, bi, i: (bi, tile(dd, i), 0)),
            pl.BlockSpec((1, d), lambda dd, bi, i: (0, 0)),
            pl.BlockSpec((1, 1, d), lambda dd, bi, i: (bi, 0, 0)),
            pl.BlockSpec((1, 1, d), lambda dd, bi, i: (bi, 0, 1)),
            pl.BlockSpec((1, SSM_BLOCKS, nsb), lambda dd, bi, i: (dd, 0, 0)),
            pl.BlockSpec((1, SSM_BLOCKS, nsb), lambda dd, bi, i: (dd, 0, 0)),
            pl.BlockSpec((1, SSM_BLOCKS, chb, nsb), lambda dd, bi, i: (dd, 0, 0, 0), pipeline_mode=single),
            pl.BlockSpec((1, SSM_BLOCKS, chb, nsb), lambda dd, bi, i: (dd, 0, 0, 0), pipeline_mode=single),
            pl.BlockSpec((1, SSM_BLOCKS, nsb, chb), lambda dd, bi, i: (dd, 0, 0, 0), pipeline_mode=single),
            pl.BlockSpec((1, SSM_BLOCKS, nsb, chb), lambda dd, bi, i: (dd, 0, 0, 0), pipeline_mode=single),
        ],
        out_specs=pl.BlockSpec((1, 1, t, d), lambda dd, bi, i: (dd, bi, tile(dd, i), 0)),
        out_shape=jax.ShapeDtypeStruct((nd, b, l, d), F32),
        scratch_shapes=[
            pltpu.VMEM((nslab, SSM_BLOCKS * pitch, LANES), F32),
            pltpu.VMEM((nslab, SSM_BLOCKS * pitch, LANES), F32),
            pltpu.VMEM((SSM_BLOCKS, nsb), F32),
            pltpu.VMEM((SSM_BLOCKS, nsb), F32),
            pltpu.VMEM((t, d), BF16),
            pltpu.VMEM((t, 1), F32),
        ],
        compiler_params=_cparams(("arbitrary", "arbitrary", "arbitrary")),
        name="s5_scan",
    )(x, norm_g.reshape(1, d), mod, mod, a_re, a_im, we_re, we_im, wc_re, wc_im)


def _s5_glu_kernel(yf_ref, yb_ref, xf_ref, g_ref, sh_ref, sc_ref, dsk_ref, wa_ref, wg_ref, ba_ref, bg_ref,
                   x_ref, g1_ref, o_ref, v_scr, inv_scr):
    @pl.when(pl.program_id(2) == 0)
    def _():
        def emit(rows, h):
            y = yf_ref[0, 0, rows, :] + yb_ref[0, 0, rows, :] + dsk_ref[...] * h
            v_scr[rows, :] = _gelu_tanh(y).astype(BF16)

        _modnorm_rows(xf_ref, g_ref, sh_ref, sc_ref, inv_scr, emit)

    v = v_scr[...]
    a = jnp.dot(v, wa_ref[...], preferred_element_type=F32) + ba_ref[...]
    g = jnp.dot(v, wg_ref[...], preferred_element_type=F32) + bg_ref[...]
    o_ref[0] = x_ref[0] + g1_ref[0] * (a * jax.nn.sigmoid(g))


def _s5_glu(y2, x, norm_g, mod, d_skip, w_glu_bf, b_glu, tm, tn):
    b, l, d = x.shape
    nj = d // tn
    return pl.pallas_call(
        _s5_glu_kernel,
        grid=(b, l // tm, nj),
        in_specs=[
            pl.BlockSpec((1, 1, tm, d), lambda bi, i, j: (0, bi, i, 0)),
            pl.BlockSpec((1, 1, tm, d), lambda bi, i, j: (1, bi, i, 0)),
            pl.BlockSpec((1, tm, d), lambda bi, i, j: (bi, i, 0)),
            pl.BlockSpec((1, d), lambda bi, i, j: (0, 0)),
            pl.BlockSpec((1, 1, d), lambda bi, i, j: (bi, 0, 0)),
            pl.BlockSpec((1, 1, d), lambda bi, i, j: (bi, 0, 1)),
            pl.BlockSpec((1, d), lambda bi, i, j: (0, 0)),
            pl.BlockSpec((d, tn), lambda bi, i, j: (0, j)),
            pl.BlockSpec((d, tn), lambda bi, i, j: (0, j + nj)),
            pl.BlockSpec((1, tn), lambda bi, i, j: (0, j)),
            pl.BlockSpec((1, tn), lambda bi, i, j: (0, j + nj)),
            pl.BlockSpec((1, tm, tn), lambda bi, i, j: (bi, i, j)),
            pl.BlockSpec((1, 1, tn), lambda bi, i, j: (bi, 0, 2 * nj + j)),
        ],
        out_specs=pl.BlockSpec((1, tm, tn), lambda bi, i, j: (bi, i, j)),
        out_shape=jax.ShapeDtypeStruct((b, l, d), F32),
        scratch_shapes=[pltpu.VMEM((tm, d), BF16), pltpu.VMEM((tm, 1), F32)],
        compiler_params=_cparams(("parallel", "parallel", "arbitrary")),
        name="s5_gelu_glu",
    )(y2, y2, x, norm_g.reshape(1, d), mod, mod, d_skip.reshape(1, d), w_glu_bf, w_glu_bf,
      b_glu.reshape(1, 2 * d), b_glu.reshape(1, 2 * d), x, mod)


def _router_kernel(x_ref, g_ref, sh_ref, sc_ref, wr_ref, hp_ref, aff_ref, hhi_scr, hlo_scr, inv_scr, *, dh):
    def emit(rows, h):
        hb = h.astype(BF16)
        hbf = hb.astype(F32)
        hhi_scr[rows, :] = hb
        hlo_scr[rows, :] = (h - hbf).astype(BF16)
        bits = lax.bitcast_convert_type(hbf, I32)
        lo = lax.shift_right_logical(bits[:, :dh], 16)
        hi = jnp.bitwise_and(bits[:, dh:], jnp.int32(-65536))
        hp_ref[rows, :] = jnp.bitwise_or(hi, lo)

    _modnorm_rows(x_ref, g_ref, sh_ref, sc_ref, inv_scr, emit)

    w = wr_ref[...]
    w_hi = w.astype(BF16)
    w_lo = (w - w_hi.astype(F32)).astype(BF16)
    nt = (((1,), (1,)), ((), ()))
    ne = w.shape[0]
    both = lax.dot_general(jnp.concatenate([w_hi, w_lo], axis=0), hhi_scr[...], nt, preferred_element_type=F32)
    logits = both[:ne] + both[ne:] + lax.dot_general(w_hi, hlo_scr[...], nt, preferred_element_type=F32)
    m = jnp.max(logits, axis=0, keepdims=True)
    ex = jnp.exp(logits - m)
    aff_ref[...] = ex / jnp.sum(ex, axis=0, keepdims=True)


def _router(x, norm_g, mod, w_router, tm):
    b, l, d = x.shape
    e = w_router.shape[1]
    dh = d // 2
    nt = l // tm
    kern = functools.partial(_router_kernel, dh=dh)
    return pl.pallas_call(
        kern,
        grid=(b, nt),
        in_specs=[
            pl.BlockSpec((1, tm, d), lambda bi, i: (bi, i, 0)),
            pl.BlockSpec((1, d), lambda bi, i: (0, 0)),
            pl.BlockSpec((1, 1, d), lambda bi, i: (bi, 0, 3)),
            pl.BlockSpec((1, 1, d), lambda bi, i: (bi, 0, 4)),
            pl.BlockSpec((e, d), lambda bi, i: (0, 0)),
        ],
        out_specs=[
            pl.BlockSpec((tm, dh), lambda bi, i: (bi * nt + i, 0)),
            pl.BlockSpec((e, tm), lambda bi, i: (0, bi * nt + i)),
        ],
        out_shape=[jax.ShapeDtypeStruct((b * l, dh), I32), jax.ShapeDtypeStruct((e, b * l), F32)],
        scratch_shapes=[pltpu.VMEM((tm, d), BF16), pltpu.VMEM((tm, d), BF16), pltpu.VMEM((tm, 1), F32)],
        compiler_params=_cparams(("parallel", "parallel")),
        name="moe_router",
    )(x, norm_g.reshape(1, d), mod, mod, w_router.T)


def _excl_cumsum_lanes(x):
    n = x.shape[1]
    lane = lax.broadcasted_iota(I32, x.shape, 1)
    inc = x
    k = 1
    while k < n:
        inc = inc + jnp.where(lane >= k, pltpu.roll(inc, k, axis=1), 0)
        k *= 2
    return inc - x


def _select_kernel(aff_ref, idx_ref, gate_ref, off_ref, pos_scr, vals_scr, racc, off_sm, sem, *, cap, ntile, tmtok):
    e, n = aff_ref.shape
    key = lax.bitcast_convert_type(aff_ref[...], I32)

    def bit_step(it, prefix):
        cand = jnp.bitwise_or(prefix, jnp.left_shift(jnp.int32(1), 30 - it))
        cnt = jnp.sum((key >= cand).astype(I32), axis=1, keepdims=True)
        return jnp.where(cnt >= cap, cand, prefix)

    thr = lax.fori_loop(0, 31, bit_step, jnp.zeros((e, 1), I32))
    gt = key > thr
    eq = key == thr
    need = cap - jnp.sum(gt.astype(I32), axis=1, keepdims=True)
    sel = jnp.logical_or(gt, jnp.logical_and(eq, _excl_cumsum_lanes(eq.astype(I32)) < need))
    sel_i = sel.astype(I32)
    pos_scr[...] = jnp.where(sel, _excl_cumsum_lanes(sel_i), -1)

    tok = lax.broadcasted_iota(I32, (e, n), 1)
    lane = lax.broadcasted_iota(I32, off_ref.shape, 1)
    off = jnp.zeros(off_ref.shape, I32)
    for b in range(1, ntile + 1):
        cnt = jnp.sum(jnp.where(tok < b * tmtok, sel_i, 0), axis=1, keepdims=True)
        off = jnp.where(lane == b, cnt, off)
    off_ref[...] = off
    off_copy = pltpu.make_async_copy(off_ref, off_sm, sem)
    off_copy.start()

    tok1 = lax.broadcasted_iota(I32, vals_scr.shape, 1)
    row = lax.broadcasted_iota(I32, vals_scr.shape, 0)
    tok_hi = jnp.right_shift(tok1, 7).astype(F32)
    tok_lo = jnp.bitwise_and(tok1, 127).astype(F32)
    off_copy.wait()

    def expert(ei, _):
        g = aff_ref[pl.ds(ei, 1), :]
        g0 = g.astype(BF16).astype(F32)
        r1 = g - g0
        g1 = r1.astype(BF16).astype(F32)
        g2 = r1 - g1
        vals_scr[...] = jnp.where(row == 0, tok_hi, jnp.where(row == 1, tok_lo, jnp.where(
            row == 2, g0, jnp.where(row == 3, g1, jnp.where(row == 4, g2, 0.0))))).astype(BF16)

        def chunk(ci, _):
            s0 = ci * SLOT_BLOCK
            slot = s0 + lax.broadcasted_iota(I32, (SLOT_BLOCK, tmtok), 0)
            racc[...] = jnp.zeros_like(racc)
            for tb in range(ntile):
                @pl.when(jnp.logical_and(off_sm[ei, tb] < s0 + SLOT_BLOCK, off_sm[ei, tb + 1] > s0))
                def _():
                    cols = pl.ds(tb * tmtok, tmtok)
                    onehot = jnp.where(pos_scr[pl.ds(ei, 1), cols] == slot, 1.0, 0.0).astype(BF16)
                    racc[...] += lax.dot_general(vals_scr[:, cols], onehot, (((1,), (1,)), ((), ())),
                                                 preferred_element_type=F32)
            r = racc[...]
            idx_ref[ei, pl.ds(ci, 1), :] = (r[0:1] * 128.0 + r[1:2]).astype(I32)
            gate_ref[ei, pl.ds(ci, 1), :] = r[2:3] + r[3:4] + r[4:5]
            return 0

        lax.fori_loop(0, cap // SLOT_BLOCK, chunk, 0)
        return 0

    lax.fori_loop(0, e, expert, 0)


def _select(aff_t, cap, ntile, tmtok):
    e, n = aff_t.shape
    nc = cap // SLOT_BLOCK
    kern = functools.partial(_select_kernel, cap=cap, ntile=ntile, tmtok=tmtok)
    idx, gate, off = pl.pallas_call(
        kern,
        out_shape=[jax.ShapeDtypeStruct((e, nc, SLOT_BLOCK), I32),
                   jax.ShapeDtypeStruct((e, nc, SLOT_BLOCK), F32),
                   jax.ShapeDtypeStruct((e, LANES), I32)],
        scratch_shapes=[pltpu.VMEM((e, n), I32), pltpu.VMEM((16, n), BF16), pltpu.VMEM((16, SLOT_BLOCK), F32),
                        pltpu.SMEM((e, LANES), I32), pltpu.SemaphoreType.DMA(())],
        compiler_params=pltpu.CompilerParams(vmem_limit_bytes=VMEM_LIMIT),
        name="moe_select",
    )(aff_t)
    return idx.reshape(e, cap), gate.reshape(e, cap), off


def _ffn_kernel(idx_sm, *refs, ngrp, cap, nf, tf, dh, rows_up, rows_down):
    hp_hbm = refs[:ngrp]
    wg_ref, wu_ref, wd_ref, o_ref, xp_scr, xs_scr, hid_scr, sem = refs[ngrp:]
    e = pl.program_id(0)
    s = pl.program_id(1)
    ne = pl.num_programs(0)
    m = ngrp * cap

    def issue_gather(expert):
        for grp in range(ngrp):
            base = (grp * ne + expert) * cap

            def body(c, _):
                pltpu.make_async_copy(hp_hbm[grp].at[pl.ds(idx_sm[base + c], 1), :],
                                      xp_scr.at[pl.ds(grp * cap + c, 1), :], sem).start()
                return 0

            lax.fori_loop(0, cap, body, 0, unroll=8)

    def wait_gather():
        pltpu.make_async_copy(hp_hbm[0].at[pl.ds(0, m), :], xp_scr, sem).wait()

    nstep = pl.num_programs(1)

    @pl.when(jnp.logical_and(e == 0, s == 0))
    def _():
        issue_gather(0)

    @pl.when(s == 0)
    def _():
        wait_gather()
        p = xp_scr[...]
        xs_scr[0] = lax.bitcast_convert_type(jnp.left_shift(p, 16), F32).astype(BF16)
        xs_scr[1] = lax.bitcast_convert_type(jnp.bitwise_and(p, jnp.int32(-65536)), F32).astype(BF16)

    nxt = jnp.minimum(e + 1, ne - 1)

    def issue_slice(first, count):
        for grp in range(ngrp):
            base = (grp * ne + nxt) * cap + first
            for c in range(count):
                pltpu.make_async_copy(hp_hbm[grp].at[pl.ds(idx_sm[base + c], 1), :],
                                      xp_scr.at[pl.ds(grp * cap + first + c, 1), :], sem).start()

    @pl.when(s < nf)
    def _():
        issue_slice(s * rows_up, rows_up)
        wg = wg_ref[0, 0].astype(BF16)
        wu = wu_ref[0, 0].astype(BF16)
        xl = xs_scr[0]
        xh = xs_scr[1]
        g = (jnp.dot(xl, wg[:dh], preferred_element_type=F32) + jnp.dot(xh, wg[dh:], preferred_element_type=F32))
        u = (jnp.dot(xl, wu[:dh], preferred_element_type=F32) + jnp.dot(xh, wu[dh:], preferred_element_type=F32))
        hid_scr[s] = (_silu(g) * u).astype(BF16)

    @pl.when(s >= nf)
    def _():
        issue_slice(nf * rows_up + (s - nf) * rows_down, rows_down)
        wd = wd_ref[0, 0].astype(BF16)
        acc = jnp.dot(hid_scr[0], wd[:tf], preferred_element_type=F32)
        for k in range(1, nf):
            acc = acc + jnp.dot(hid_scr[k], wd[k * tf:(k + 1) * tf], preferred_element_type=F32)
        for grp in range(ngrp):
            o_ref[grp, 0] = acc[grp * cap:(grp + 1) * cap]

    @pl.when(jnp.logical_and(e == ne - 1, s == nstep - 1))
    def _():
        wait_gather()


def _ffn(idx_flat, hps, w_gate, w_up, w_down, layer, cap, tf, tn):
    ngrp = len(hps)
    n, dh = hps[0].shape
    _, ne, d, f = w_gate.shape
    nf = f // tf
    nn = d // tn
    m = ngrp * cap
    assert all(hp.shape[0] >= m for hp in hps)
    assert (cap * tf) % (f + d) == 0 and (cap * tn) % (f + d) == 0
    kern = functools.partial(_ffn_kernel, ngrp=ngrp, cap=cap, nf=nf, tf=tf, dh=dh,
                             rows_up=cap * tf // (f + d), rows_down=cap * tn // (f + d))
    grid_spec = pltpu.PrefetchScalarGridSpec(
        num_scalar_prefetch=1,
        grid=(ne, nf + nn),
        in_specs=[pl.BlockSpec(memory_space=pl.ANY)] * ngrp + [
            pl.BlockSpec((1, 1, d, tf), lambda e, s, idx: (layer, e, 0, jnp.minimum(s, nf - 1))),
            pl.BlockSpec((1, 1, d, tf), lambda e, s, idx: (layer, e, 0, jnp.minimum(s, nf - 1))),
            pl.BlockSpec((1, 1, f, tn), lambda e, s, idx: (layer, e, 0, jnp.maximum(s - nf, 0))),
        ],
        out_specs=pl.BlockSpec((ngrp, 1, cap, tn), lambda e, s, idx: (0, e, 0, jnp.maximum(s - nf, 0))),
        scratch_shapes=[
            pltpu.VMEM((m, dh), I32),
            pltpu.VMEM((2, m, dh), BF16),
            pltpu.VMEM((nf, m, tf), BF16),
            pltpu.SemaphoreType.DMA(()),
        ],
    )
    return pl.pallas_call(
        kern,
        grid_spec=grid_spec,
        out_shape=jax.ShapeDtypeStruct((ngrp, ne, cap, d), F32),
        compiler_params=_cparams(("arbitrary", "arbitrary")),
        name="moe_ffn",
    )(idx_flat, *hps, w_gate, w_up, w_down)


def _worklist(off, ne, ntile, nblk):
    bs = SLOT_BLOCK
    w_max = ne * (nblk + ntile - 1) + ntile
    lo = off[:, :ntile].T.reshape(-1)
    hi = off[:, 1:ntile + 1].T.reshape(-1)
    b0 = jnp.minimum(lo // bs, nblk - 1)
    nb = jnp.where(hi > lo, (hi + bs - 1) // bs - b0, 0)
    first_e = (jnp.arange(ntile * ne, dtype=I32) % ne) == 0
    nb = jnp.where(first_e, jnp.maximum(nb, 1), nb)
    end = jnp.cumsum(nb)
    total = end[-1]
    w = jnp.arange(w_max, dtype=I32)
    wc = jnp.minimum(w, total - 1)
    pair = jnp.sum((end[None, :] <= wc[:, None]).astype(I32), axis=1)
    table = jnp.stack([end - nb, b0, lo, hi], axis=1)
    onehot = pair[:, None] == jnp.arange(ntile * ne, dtype=I32)[None, :]
    got = jnp.sum(jnp.where(onehot[:, :, None], table[None], 0), axis=1)
    tile = pair // ne
    exp = pair % ne
    blk = got[:, 1] + (wc - got[:, 0])
    valid = w < total
    rlo = jnp.where(valid, jnp.clip(got[:, 2] - blk * bs, 0, bs), 0)
    rhi = jnp.where(valid, jnp.clip(got[:, 3] - blk * bs, 0, bs), 0)
    first = jnp.concatenate([jnp.ones((1,), bool), tile[1:] != tile[:-1]])
    last = jnp.concatenate([tile[1:] != tile[:-1], jnp.ones((1,), bool)])
    to_i = lambda a: a.astype(I32)
    return to_i(tile), to_i(exp), to_i(blk), to_i(rlo), to_i(rhi), to_i(first), to_i(last)


def _combine_kernel(tile_sm, exp_sm, blk_sm, rlo_sm, rhi_sm, first_sm, last_sm, tq_sm, tu_sm, gate_sm,
                    src_ref, x_ref, g2_ref, fg_ref, o_ref, acc_scr, *, cap, final):
    w = pl.program_id(0)

    @pl.when(first_sm[w] == 1)
    def _():
        acc_scr[...] = jnp.zeros_like(acc_scr)

    slot0 = exp_sm[w] * cap + blk_sm[w] * SLOT_BLOCK

    rlo = rlo_sm[w]
    rhi = rhi_sm[w]
    body0 = jnp.minimum(((rlo + COMBINE_ROWS - 1) // COMBINE_ROWS) * COMBINE_ROWS, rhi)
    body1 = jnp.maximum((rhi // COMBINE_ROWS) * COMBINE_ROWS, body0)

    def row(r, _):
        q, u = tq_sm[slot0 + r], tu_sm[slot0 + r]
        acc_scr[q, pl.ds(u, 1), :] = (acc_scr[q, pl.ds(u, 1), :]
                                      + src_ref[0, 0, r // COMBINE_ROWS, pl.ds(r % COMBINE_ROWS, 1), :]
                                      * gate_sm[slot0 + r])
        return 0

    def group(gi, _):
        sq = body0 // COMBINE_ROWS + gi
        s0 = slot0 + sq * COMBINE_ROWS
        qs = [tq_sm[s0 + u] for u in range(COMBINE_ROWS)]
        us = [tu_sm[s0 + u] for u in range(COMBINE_ROWS)]
        new = [acc_scr[qs[u], pl.ds(us[u], 1), :] + src_ref[0, 0, sq, u:u + 1, :] * gate_sm[s0 + u]
               for u in range(COMBINE_ROWS)]
        for u in range(COMBINE_ROWS):
            acc_scr[qs[u], pl.ds(us[u], 1), :] = new[u]
        return 0

    lax.fori_loop(rlo, body0, row, 0)
    lax.fori_loop(0, (body1 - body0) // COMBINE_ROWS, group, 0)
    lax.fori_loop(body1, rhi, row, 0)

    @pl.when(last_sm[w] == 1)
    def _():
        xn = x_ref[0] + g2_ref[0] * acc_scr[...]
        if final:
            ms = jnp.mean(xn * xn, axis=-1, keepdims=True)
            xn = xn * lax.rsqrt(ms + RMS_EPS) * fg_ref[...]
        o_ref[0] = xn


def _combine(work, idx_flat_g, gate_flat_g, outs, grp, x, mod, final_g, tmtok, final):
    b, l, d = x.shape
    ngrp, ne, cap = outs.shape[:3]
    tpb = l // tmtok
    w_max = work[0].shape[0]
    rs = COMBINE_ROWS
    kern = functools.partial(_combine_kernel, cap=cap, final=final)
    tloc = idx_flat_g % tmtok
    outs_r = outs.reshape(ngrp, ne, cap // rs, rs, d)
    x_r = x.reshape(b, l // rs, rs, d)
    tile_rows = tmtok // rs
    grid_spec = pltpu.PrefetchScalarGridSpec(
        num_scalar_prefetch=10,
        grid=(w_max,),
        in_specs=[
            pl.BlockSpec((1, 1, SLOT_BLOCK // rs, rs, d),
                         lambda w, ti, ex, bl, *_: (grp, ex[w], bl[w], 0, 0)),
            pl.BlockSpec((1, tile_rows, rs, d), lambda w, ti, *_: (ti[w] // tpb, ti[w] % tpb, 0, 0),
                         pipeline_mode=pl.Buffered(1)),
            pl.BlockSpec((1, 1, d), lambda w, ti, *_: (ti[w] // tpb, 0, 5)),
            pl.BlockSpec((1, d), lambda w, *_: (0, 0)),
        ],
        out_specs=pl.BlockSpec((1, tile_rows, rs, d), lambda w, ti, *_: (ti[w] // tpb, ti[w] % tpb, 0, 0)),
        scratch_shapes=[pltpu.VMEM((tile_rows, rs, d), F32)],
    )
    out = pl.pallas_call(
        kern,
        grid_spec=grid_spec,
        out_shape=jax.ShapeDtypeStruct((b, l // rs, rs, d), F32),
        compiler_params=_cparams(("arbitrary",)),
        name="moe_combine",
    )(*work, tloc // rs, tloc % rs, gate_flat_g, outs_r, x_r, mod, final_g.reshape(1, d))
    return out.reshape(b, l, d)


def _pick(n, pref):
    t = min(pref, n)
    while n % t:
        t //= 2
    return t


def _moe_layer(xs, mods, norm_g, w_router, w_gate, w_up, w_down, layer, final_g, final):
    ne = w_router.shape[1]
    hps, idxs, gates, works = [], [], [], []
    caps = set()
    for x, mod in zip(xs, mods):
        b, l, d = x.shape
        n = b * l
        cap = EC_CAPACITY_FACTOR * n // ne
        caps.add(cap)
        tmtok = _pick(l, 1024)
        ntile = n // tmtok
        hp, aff_t = _router(x, norm_g, mod, w_router, _pick(l, 512))
        idx, gate, off = _select(aff_t, cap, ntile, tmtok)
        hps.append(hp)
        idxs.append(idx.reshape(-1))
        gates.append(gate.reshape(-1))
        works.append(_worklist(off, ne, ntile, cap // SLOT_BLOCK))
    assert len(caps) == 1, "request groups must have equal expert capacity"
    cap = caps.pop()
    f = w_gate.shape[3]
    outs = _ffn(jnp.concatenate(idxs), hps, w_gate, w_up, w_down, layer, cap,
                _pick(f, 256), _pick(xs[0].shape[2], 256))
    new = []
    for grp, (x, mod) in enumerate(zip(xs, mods)):
        tmtok = _pick(x.shape[1], 1024)
        new.append(_combine(works[grp], idxs[grp], gates[grp], outs, grp, x, mod, final_g, tmtok, final))
    return new


def kernel(x_prompt, x_sample, c_prompt, c_sample, ada_w, ada_b, norm_mix_g, norm_ffn_g, final_norm_g, conv_w_in, conv_b_in, conv_w_dw, conv_b_dw, conv_ln_g, conv_ln_b, conv_w_out, conv_b_out, ssm_lambda_re, ssm_lambda_im, ssm_log_step, ssm_b_re, ssm_b_im, ssm_c_re, ssm_c_im, ssm_d, ssm_w_glu, ssm_b_glu, moe_w_router, moe_w_gate, moe_w_up, moe_w_down):
    depth = ada_w.shape[0]
    d = x_prompt.shape[-1]
    bp, bs = x_prompt.shape[0], x_sample.shape[0]
    nrow = 16 * ((bp + bs + 15) // 16)
    c_all = jnp.concatenate([c_prompt, c_sample, jnp.zeros((nrow - bp - bs, d), F32)], axis=0)
    mod_all = _ada(c_all, ada_w, ada_b)

    xs = [x_prompt, x_sample]
    for i in range(depth):
        mods = [mod_all[i, :bp].reshape(bp, 1, 6 * d), mod_all[i, bp:bp + bs].reshape(bs, 1, 6 * d)]
        j = i // 2
        if i % 2 == 0:
            w_in = conv_w_in[j].astype(BF16)
            w_out = conv_w_out[j].astype(BF16)
            nxt = []
            for x, mod in zip(xs, mods):
                l = x.shape[1]
                u = _conv_in(x, norm_mix_g[i], mod, w_in, conv_b_in[j], _pick(l, 512), _pick(d, 1024))
                nxt.append(_conv_out(u, x, mod, conv_w_dw[j], conv_b_dw[j], conv_ln_g[j], conv_ln_b[j],
                                     w_out, conv_b_out[j], _pick(l, 512), _pick(d, 1024)))
            xs = nxt
        else:
            lay = _s5_layout(ssm_lambda_re[j], ssm_lambda_im[j], ssm_log_step[j], ssm_b_re[j], ssm_b_im[j],
                             ssm_c_re[j], ssm_c_im[j])
            w_glu = ssm_w_glu[j].astype(BF16)
            nxt = []
            for x, mod in zip(xs, mods):
                l = x.shape[1]
                y2 = _s5_scan(x, norm_mix_g[i], mod, lay, _pick(l, 256))
                nxt.append(_s5_glu(y2, x, norm_mix_g[i], mod, ssm_d[j], w_glu, ssm_b_glu[j],
                                   _pick(l, 512), _pick(d, 1024)))
            xs = nxt
        xs = _moe_layer(xs, mods, norm_ffn_g[i], moe_w_router[i], moe_w_gate, moe_w_up, moe_w_down, i,
                        final_norm_g, final=(i == depth - 1))
    return (xs[0], xs[1])
```

```python
import functools
import math

import jax
import jax.numpy as jnp
from jax import lax
from jax.experimental import pallas as pl
from jax.experimental.pallas import tpu as pltpu

F32 = jnp.float32
BF16 = jnp.bfloat16
I32 = jnp.int32
HIGHEST = lax.Precision.HIGHEST

RMS_EPS = 1e-6
LN_EPS = 1e-5
CONV_WIDTH = 31
CONV_PAD = (CONV_WIDTH - 1) // 2
CONV_HALO = 16
SSM_GROUP = 16
SSM_STATE = 64
SSM_BLOCKS = 8
S5_LAG = 4
N_EXPERTS = 16
EC_CAPACITY_FACTOR = 2
LANES = 128
SLOT_BLOCK = 128
COMBINE_ROWS = 8
COMBINE_ITEMS = 4
ROW_CHUNK = 16

VMEM_LIMIT = 56 * 1024 * 1024


def _cparams(sem, vmem=VMEM_LIMIT):
    return pltpu.CompilerParams(dimension_semantics=sem, vmem_limit_bytes=vmem)


def _modnorm_rows(x_ref, g_ref, sh_ref, sc_ref, inv_scr, emit):
    x = x_ref[0]
    inv_scr[...] = lax.rsqrt(jnp.mean(x * x, axis=-1, keepdims=True) + RMS_EPS)
    gain = g_ref[...] * (1.0 + sc_ref[0])
    shift = sh_ref[0]

    def chunk(r, _):
        rows = pl.ds(pl.multiple_of(r * ROW_CHUNK, ROW_CHUNK), ROW_CHUNK)
        emit(rows, x_ref[0, rows, :] * inv_scr[rows, :] * gain + shift)
        return 0

    lax.fori_loop(0, x.shape[0] // ROW_CHUNK, chunk, 0, unroll=2)


def _silu(x):
    return x * jax.nn.sigmoid(x)


def _gelu_tanh(x):
    c = math.sqrt(2.0 / math.pi)
    return 0.5 * x * (1.0 + jnp.tanh(c * (x + 0.044715 * (x * x * x))))


def _ada_kernel(c_ref, w_ref, b_ref, o_ref):
    c = c_ref[...]
    cond = _silu(c)
    c_hi = cond.astype(BF16)
    c_lo = (cond - c_hi.astype(F32)).astype(BF16)
    w = w_ref[0]
    w_hi = w.astype(BF16)
    w_lo = (w - w_hi.astype(F32)).astype(BF16)
    o_ref[0] = (jnp.dot(c_hi, w_hi, preferred_element_type=F32) + jnp.dot(c_lo, w_hi, preferred_element_type=F32)
                + jnp.dot(c_hi, w_lo, preferred_element_type=F32) + b_ref[0])


def _ada(c_all, ada_w, ada_b):
    depth, d, n6 = ada_w.shape
    rows = c_all.shape[0]
    tn = _pick(n6, 1024)
    return pl.pallas_call(
        _ada_kernel,
        grid=(depth, n6 // tn),
        in_specs=[
            pl.BlockSpec((rows, d), lambda i, j: (0, 0)),
            pl.BlockSpec((1, d, tn), lambda i, j: (i, 0, j)),
            pl.BlockSpec((1, 1, tn), lambda i, j: (i, 0, j)),
        ],
        out_specs=pl.BlockSpec((1, rows, tn), lambda i, j: (i, 0, j)),
        out_shape=jax.ShapeDtypeStruct((depth, rows, n6), F32),
        compiler_params=_cparams(("parallel", "parallel")),
        name="ada_mod",
    )(c_all, ada_w, ada_b.reshape(depth, 1, n6))


def _conv_in_kernel(x_ref, g_ref, sh_ref, sc_ref, wa_ref, wg_ref, ba_ref, bg_ref, o_ref, h_scr, inv_scr):
    @pl.when(pl.program_id(2) == 0)
    def _():
        def emit(rows, h):
            h_scr[rows, :] = h.astype(BF16)

        _modnorm_rows(x_ref, g_ref, sh_ref, sc_ref, inv_scr, emit)

    h = h_scr[...]
    a = jnp.dot(h, wa_ref[...], preferred_element_type=F32) + ba_ref[...]
    g = jnp.dot(h, wg_ref[...], preferred_element_type=F32) + bg_ref[...]
    o_ref[0] = a * jax.nn.sigmoid(g)


def _conv_in(x, norm_g, mod, w_in_bf, b_in, tm, tn):
    b, l, d = x.shape
    nj = d // tn
    return pl.pallas_call(
        _conv_in_kernel,
        grid=(b, l // tm, nj),
        in_specs=[
            pl.BlockSpec((1, tm, d), lambda bi, i, j: (bi, i, 0)),
            pl.BlockSpec((1, d), lambda bi, i, j: (0, 0)),
            pl.BlockSpec((1, 1, d), lambda bi, i, j: (bi, 0, 0)),
            pl.BlockSpec((1, 1, d), lambda bi, i, j: (bi, 0, 1)),
            pl.BlockSpec((d, tn), lambda bi, i, j: (0, j)),
            pl.BlockSpec((d, tn), lambda bi, i, j: (0, j + nj)),
            pl.BlockSpec((1, tn), lambda bi, i, j: (0, j)),
            pl.BlockSpec((1, tn), lambda bi, i, j: (0, j + nj)),
        ],
        out_specs=pl.BlockSpec((1, tm, tn), lambda bi, i, j: (bi, i, j)),
        out_shape=jax.ShapeDtypeStruct((b, l, d), F32),
        scratch_shapes=[pltpu.VMEM((tm, d), BF16), pltpu.VMEM((tm, 1), F32)],
        compiler_params=_cparams(("parallel", "parallel", "arbitrary")),
        name="conv_in_glu",
    )(x, norm_g.reshape(1, d), mod, mod, w_in_bf, w_in_bf, b_in.reshape(1, 2 * d), b_in.reshape(1, 2 * d))


def _conv_out_kernel(u_ref, up_ref, un_ref, wdw_ref, bdw_ref, lg_ref, lb_ref, wo_ref, bo_ref, x_ref, g1_ref,
                     o_ref, ext_scr, c_scr, v_scr, mu_scr, inv_scr, *, tm, d, rc, cc):
    i = pl.program_id(1)
    ni = pl.num_programs(1)

    ncol = d // cc

    @pl.when(pl.program_id(2) == 0)
    def _():
        for c in range(ncol):
            cols = pl.ds(c * cc, cc)
            ext_scr[c, pl.ds(0, CONV_HALO), :] = jnp.where(i > 0, up_ref[0, :, cols], 0.0)
            ext_scr[c, pl.ds(CONV_HALO, tm), :] = u_ref[0, :, cols]
            ext_scr[c, pl.ds(CONV_HALO + tm, CONV_HALO), :] = jnp.where(i < ni - 1, un_ref[0, :, cols], 0.0)

        def col_chunk(c, _):
            for r0 in range(0, tm, rc):
                acc = jnp.zeros((rc, cc), F32) + bdw_ref[c]
                for k in range(CONV_WIDTH):
                    seg = ext_scr.at[c][pl.ds(r0 + (CONV_HALO - CONV_PAD + k), rc, stride=1), :]
                    acc = acc + seg * wdw_ref[c, pl.ds(k, 1), :]
                c_scr[c, pl.ds(r0, rc), :] = acc
            return 0

        lax.fori_loop(0, ncol, col_chunk, 0)

        def rows_of(r):
            return pl.ds(pl.multiple_of(r * ROW_CHUNK, ROW_CHUNK), ROW_CHUNK)

        def mean_chunk(r, _):
            rows = rows_of(r)
            tot = functools.reduce(lambda a, v: a + v, [c_scr[c, rows, :] for c in range(ncol)])
            mu_scr[rows, :] = jnp.sum(tot, axis=-1, keepdims=True) * (1.0 / d)
            return 0

        def var_chunk(r, _):
            rows = rows_of(r)
            mu = mu_scr[rows, :]
            dev = [c_scr[c, rows, :] - mu for c in range(ncol)]
            tot = functools.reduce(lambda a, v: a + v, [v * v for v in dev])
            inv_scr[rows, :] = lax.rsqrt(jnp.sum(tot, axis=-1, keepdims=True) * (1.0 / d) + LN_EPS)
            return 0

        def norm_chunk(r, _):
            rows = rows_of(r)
            mu = mu_scr[rows, :]
            inv = inv_scr[rows, :]
            for c in range(ncol):
                cols = pl.ds(c * cc, cc)
                y = (c_scr[c, rows, :] - mu) * inv * lg_ref[:, cols] + lb_ref[:, cols]
                v_scr[rows, cols] = _silu(y).astype(BF16)
            return 0

        for body, unroll in ((mean_chunk, 16), (var_chunk, 4), (norm_chunk, 4)):
            lax.fori_loop(0, tm // ROW_CHUNK, body, 0, unroll=min(unroll, tm // ROW_CHUNK))

    o = jnp.dot(v_scr[...], wo_ref[...], preferred_element_type=F32) + bo_ref[...]
    o_ref[0] = x_ref[0] + g1_ref[0] * o


def _conv_out(u, x, mod, w_dw, b_dw, ln_g, ln_b, w_out_bf, b_out, tm, tn):
    b, l, d = x.shape
    nj = d // tn
    hb = tm // CONV_HALO
    nhb = l // CONV_HALO
    rc = min(64, tm)
    cc = min(LANES, d)
    ncol = d // cc
    w_dw_p = jnp.concatenate([w_dw, jnp.zeros((1, d), w_dw.dtype)], axis=0)
    w_dw_p = w_dw_p.reshape(CONV_WIDTH + 1, ncol, cc).transpose(1, 0, 2)
    kern = functools.partial(_conv_out_kernel, tm=tm, d=d, rc=rc, cc=cc)
    return pl.pallas_call(
        kern,
        grid=(b, l // tm, nj),
        in_specs=[
            pl.BlockSpec((1, tm, d), lambda bi, i, j: (bi, i, 0)),
            pl.BlockSpec((1, CONV_HALO, d), lambda bi, i, j: (bi, jnp.maximum(i * hb - 1, 0), 0)),
            pl.BlockSpec((1, CONV_HALO, d), lambda bi, i, j: (bi, jnp.minimum((i + 1) * hb, nhb - 1), 0)),
            pl.BlockSpec((ncol, CONV_WIDTH + 1, cc), lambda bi, i, j: (0, 0, 0)),
            pl.BlockSpec((ncol, 1, cc), lambda bi, i, j: (0, 0, 0)),
            pl.BlockSpec((1, d), lambda bi, i, j: (0, 0)),
            pl.BlockSpec((1, d), lambda bi, i, j: (0, 0)),
            pl.BlockSpec((d, tn), lambda bi, i, j: (0, j)),
            pl.BlockSpec((1, tn), lambda bi, i, j: (0, j)),
            pl.BlockSpec((1, tm, tn), lambda bi, i, j: (bi, i, j)),
            pl.BlockSpec((1, 1, tn), lambda bi, i, j: (bi, 0, 2 * nj + j)),
        ],
        out_specs=pl.BlockSpec((1, tm, tn), lambda bi, i, j: (bi, i, j)),
        out_shape=jax.ShapeDtypeStruct((b, l, d), F32),
        scratch_shapes=[
            pltpu.VMEM((ncol, tm + 2 * CONV_HALO, cc), F32),
            pltpu.VMEM((ncol, tm, cc), F32),
            pltpu.VMEM((tm, d), BF16),
            pltpu.VMEM((tm, 1), F32),
            pltpu.VMEM((tm, 1), F32),
        ],
        compiler_params=_cparams(("parallel", "parallel", "arbitrary")),
        name="conv_dw_ln_out",
    )(u, u, u, w_dw_p, b_dw.reshape(ncol, 1, cc), ln_g.reshape(1, d), ln_b.reshape(1, d), w_out_bf,
      b_out.reshape(1, d), x, mod)


def _zoh_discretise(lam_re, lam_im, log_step, b_re, b_im):
    dt = jnp.exp(log_step)[..., None]
    mag = jnp.exp(lam_re * dt)
    a_re = mag * jnp.cos(lam_im * dt)
    a_im = mag * jnp.sin(lam_im * dt)
    nr = a_re - 1.0
    ni = a_im
    den = lam_re * lam_re + lam_im * lam_im
    k_re = (nr * lam_re + ni * lam_im) / den
    k_im = (ni * lam_re - nr * lam_im) / den
    bb_re = k_re[..., None] * b_re - k_im[..., None] * b_im
    bb_im = k_re[..., None] * b_im + k_im[..., None] * b_re
    return a_re, a_im, bb_re, bb_im


def _s5_layout(lam_re, lam_im, log_step, b_re, b_im, c_re, c_im):
    a_re, a_im, bb_re, bb_im = _zoh_discretise(lam_re, lam_im, log_step, b_re, b_im)
    nd, g, p = a_re.shape
    c = b_re.shape[-1]
    gpb = g // SSM_BLOCKS
    chb, nsb = gpb * c, gpb * p

    def rows_gc(bb):
        return bb.reshape(nd, SSM_BLOCKS, gpb, p, c).transpose(0, 1, 2, 4, 3).reshape(nd, SSM_BLOCKS, chb, p)

    def rows_gp(cc):
        return cc.reshape(nd, SSM_BLOCKS, gpb, c, p).transpose(0, 1, 2, 4, 3).reshape(nd, SSM_BLOCKS, nsb, c)

    def kern(er_ref, ei_ref, cr_ref, ci_ref, wer_ref, wei_ref, wcr_ref, wci_ref):
        def spread(x, width, rows_per_group, cols_per_group):
            k = x.shape[1]
            rep = (lax.broadcasted_iota(I32, (k, width), 1) % k == lax.broadcasted_iota(I32, (k, width), 0))
            full = jnp.dot(x.astype(BF16), rep.astype(BF16), preferred_element_type=F32)
            rg = lax.broadcasted_iota(I32, full.shape, 0) // rows_per_group
            cg = lax.broadcasted_iota(I32, full.shape, 1) // cols_per_group
            return jnp.where(rg == cg, full, 0.0).astype(BF16)

        wer_ref[0, 0] = spread(er_ref[0, 0], nsb, c, p)
        wei_ref[0, 0] = spread(ei_ref[0, 0], nsb, c, p)
        wcr_ref[0, 0] = spread(cr_ref[0, 0], chb, p, c)
        wci_ref[0, 0] = spread(ci_ref[0, 0], chb, p, c)

    blk = lambda r, k: pl.BlockSpec((1, 1, r, k), lambda dd, s: (dd, s, 0, 0))
    we_re, we_im, wc_re, wc_im = pl.pallas_call(
        kern,
        grid=(nd, SSM_BLOCKS),
        in_specs=[blk(chb, p), blk(chb, p), blk(nsb, c), blk(nsb, c)],
        out_specs=[blk(chb, nsb), blk(chb, nsb), blk(nsb, chb), blk(nsb, chb)],
        out_shape=[jax.ShapeDtypeStruct((nd, SSM_BLOCKS, chb, nsb), BF16)] * 2
        + [jax.ShapeDtypeStruct((nd, SSM_BLOCKS, nsb, chb), BF16)] * 2,
        compiler_params=_cparams(("parallel", "parallel")),
        name="s5_weight_layout",
    )(rows_gc(bb_re), rows_gc(bb_im), rows_gp(c_re), rows_gp(c_im))
    return (a_re.reshape(nd, SSM_BLOCKS, nsb), a_im.reshape(nd, SSM_BLOCKS, nsb), we_re, we_im, wc_re, wc_im)


def _s5_kernel(x_ref, g_ref, sh_ref, sc_ref, ar_ref, ai_ref, wer_ref, wei_ref, wcr_ref, wci_ref,
               o_ref, sre, sim, cre, cim, h_scr, inv_scr, *, t, chb, nsb, pitch):
    dirn = pl.program_id(0)
    nslab = nsb // LANES

    @pl.when(jnp.logical_and(jnp.logical_and(dirn == 0, pl.program_id(1) == 0), pl.program_id(2) == 0))
    def _():
        sre[...] = jnp.zeros_like(sre)
        sim[...] = jnp.zeros_like(sim)

    @pl.when(pl.program_id(2) == 0)
    def _():
        cre[...] = jnp.zeros_like(cre)
        cim[...] = jnp.zeros_like(cim)

    def emit(rows, h):
        h_scr[rows, :] = h.astype(BF16)

    _modnorm_rows(x_ref, g_ref, sh_ref, sc_ref, inv_scr, emit)

    def base(s):
        return s * pitch + (S5_LAG if s % 2 else 0)

    for s in range(SSM_BLOCKS):
        hs = h_scr[:, s * chb:(s + 1) * chb]
        bur = jnp.dot(hs, wer_ref[0, s], preferred_element_type=F32)
        bui = jnp.dot(hs, wei_ref[0, s], preferred_element_type=F32)
        for j in range(nslab):
            sre[j, pl.ds(base(s), t), :] = bur[:, j * LANES:(j + 1) * LANES]
            sim[j, pl.ds(base(s), t), :] = bui[:, j * LANES:(j + 1) * LANES]

    ar = [ar_ref[0, :, j * LANES:(j + 1) * LANES] for j in range(nslab)]
    ai = [ai_ref[0, :, j * LANES:(j + 1) * LANES] for j in range(nslab)]
    odd = jnp.bitwise_and(lax.broadcasted_iota(I32, (SSM_BLOCKS, LANES), 0), 1)
    nrow = t + S5_LAG

    def make_step(active):
        def step(k, carry):
            xr, xi = carry
            row = jnp.where(dirn == 0, k, nrow - 1 - k)
            rows = pl.ds(row, SSM_BLOCKS, stride=pitch)
            nr, ni = [], []
            for j in range(nslab):
                br = sre.at[j][rows, :]
                bi = sim.at[j][rows, :]
                r = ar[j] * xr[j] - ai[j] * xi[j] + br
                im = ar[j] * xi[j] + ai[j] * xr[j] + bi
                sre.at[j][rows, :] = r
                sim.at[j][rows, :] = im
                if active is not None:
                    r = jnp.where(active, r, xr[j])
                    im = jnp.where(active, im, xi[j])
                nr.append(r)
                ni.append(im)
            return tuple(nr), tuple(ni)
        return step

    x0 = (tuple(cre[:, j * LANES:(j + 1) * LANES] for j in range(nslab)),
          tuple(cim[:, j * LANES:(j + 1) * LANES] for j in range(nslab)))
    x1 = lax.fori_loop(0, S5_LAG, make_step(odd == dirn), x0)
    x2 = lax.fori_loop(S5_LAG, t, make_step(None), x1, unroll=4)
    xr, xi = lax.fori_loop(t, nrow, make_step(odd != dirn), x2)
    for j in range(nslab):
        cre[:, j * LANES:(j + 1) * LANES] = xr[j]
        cim[:, j * LANES:(j + 1) * LANES] = xi[j]

    for s in range(SSM_BLOCKS):
        sr = jnp.concatenate([sre[j, pl.ds(base(s), t), :] for j in range(nslab)], axis=1).astype(BF16)
        si = jnp.concatenate([sim[j, pl.ds(base(s), t), :] for j in range(nslab)], axis=1).astype(BF16)
        y = (jnp.dot(sr, wcr_ref[0, s], preferred_element_type=F32)
             - jnp.dot(si, wci_ref[0, s], preferred_element_type=F32))
        o_ref[0, 0, :, s * chb:(s + 1) * chb] = y


def _s5_scan(x, norm_g, mod, lay, t):
    a_re, a_im, we_re, we_im, wc_re, wc_im = lay
    b, l, d = x.shape
    nd = a_re.shape[0]
    chb = d // SSM_BLOCKS
    nsb = a_re.shape[-1]
    nslab = nsb // LANES
    nt = l // t
    pitch = t + S5_LAG
    kern = functools.partial(_s5_kernel, t=t, chb=chb, nsb=nsb, pitch=pitch)

    def tile(dd, i):
        return i + dd * (nt - 1 - 2 * i)

    single = pl.Buffered(1)
    return pl.pallas_call(
        kern,
        grid=(nd, b, nt),
        in_specs=[
            pl.BlockSpec((1, t, d), lambda dd, bi, i: (bi, tile(dd, i), 0)),
            pl.BlockSpec((1, d), lambda dd, bi, i: (0, 0)),
            pl.BlockSpec((1, 1, d), lambda dd, bi, i: (bi, 0, 0)),
            pl.BlockSpec((1, 1, d), lambda dd, bi, i: (bi, 0, 1)),
            pl.BlockSpec((1, SSM_BLOCKS, nsb), lambda dd, bi, i: (dd, 0, 0)),
            pl.BlockSpec((1, SSM_BLOCKS, nsb), lambda dd, bi, i: (dd, 0, 0)),
            pl.BlockSpec((1, SSM_BLOCKS, chb, nsb), lambda dd, bi, i: (dd, 0, 0, 0), pipeline_mode=single),
            pl.BlockSpec((1, SSM_BLOCKS, chb, nsb), lambda dd, bi, i: (dd, 0, 0, 0), pipeline_mode=single),
            pl.BlockSpec((1, SSM_BLOCKS, nsb, chb), lambda dd, bi, i: (dd, 0, 0, 0), pipeline_mode=single),
            pl.BlockSpec((1, SSM_BLOCKS, nsb, chb), lambda dd, bi, i: (dd, 0, 0, 0), pipeline_mode=single),
        ],
        out_specs=pl.BlockSpec((1, 1, t, d), lambda dd, bi, i: (dd, bi, tile(dd, i), 0)),
        out_shape=jax.ShapeDtypeStruct((nd, b, l, d), F32),
        scratch_shapes=[
            pltpu.VMEM((nslab, SSM_BLOCKS * pitch, LANES), F32),
            pltpu.VMEM((nslab, SSM_BLOCKS * pitch, LANES), F32),
            pltpu.VMEM((SSM_BLOCKS, nsb), F32),
            pltpu.VMEM((SSM_BLOCKS, nsb), F32),
            pltpu.VMEM((t, d), BF16),
            pltpu.VMEM((t, 1), F32),
        ],
        compiler_params=_cparams(("arbitrary", "arbitrary", "arbitrary")),
        name="s5_scan",
    )(x, norm_g.reshape(1, d), mod, mod, a_re, a_im, we_re, we_im, wc_re, wc_im)


def _s5_glu_kernel(yf_ref, yb_ref, xf_ref, g_ref, sh_ref, sc_ref, dsk_ref, wa_ref, wg_ref, ba_ref, bg_ref,
                   x_ref, g1_ref, o_ref, v_scr, inv_scr):
    @pl.when(pl.program_id(2) == 0)
    def _():
        def emit(rows, h):
            y = yf_ref[0, 0, rows, :] + yb_ref[0, 0, rows, :] + dsk_ref[...] * h
            v_scr[rows, :] = _gelu_tanh(y).astype(BF16)

        _modnorm_rows(xf_ref, g_ref, sh_ref, sc_ref, inv_scr, emit)

    v = v_scr[...]
    a = jnp.dot(v, wa_ref[...], preferred_element_type=F32) + ba_ref[...]
    g = jnp.dot(v, wg_ref[...], preferred_element_type=F32) + bg_ref[...]
    o_ref[0] = x_ref[0] + g1_ref[0] * (a * jax.nn.sigmoid(g))


def _s5_glu(y2, x, norm_g, mod, d_skip, w_glu_bf, b_glu, tm, tn):
    b, l, d = x.shape
    nj = d // tn
    return pl.pallas_call(
        _s5_glu_kernel,
        grid=(b, l // tm, nj),
        in_specs=[
            pl.BlockSpec((1, 1, tm, d), lambda bi, i, j: (0, bi, i, 0)),
            pl.BlockSpec((1, 1, tm, d), lambda bi, i, j: (1, bi, i, 0)),
            pl.BlockSpec((1, tm, d), lambda bi, i, j: (bi, i, 0)),
            pl.BlockSpec((1, d), lambda bi, i, j: (0, 0)),
            pl.BlockSpec((1, 1, d), lambda bi, i, j: (bi, 0, 0)),
            pl.BlockSpec((1, 1, d), lambda bi, i, j: (bi, 0, 1)),
            pl.BlockSpec((1, d), lambda bi, i, j: (0, 0)),
            pl.BlockSpec((d, tn), lambda bi, i, j: (0, j)),
            pl.BlockSpec((d, tn), lambda bi, i, j: (0, j + nj)),
            pl.BlockSpec((1, tn), lambda bi, i, j: (0, j)),
            pl.BlockSpec((1, tn), lambda bi, i, j: (0, j + nj)),
            pl.BlockSpec((1, tm, tn), lambda bi, i, j: (bi, i, j)),
            pl.BlockSpec((1, 1, tn), lambda bi, i, j: (bi, 0, 2 * nj + j)),
        ],
        out_specs=pl.BlockSpec((1, tm, tn), lambda bi, i, j: (bi, i, j)),
        out_shape=jax.ShapeDtypeStruct((b, l, d), F32),
        scratch_shapes=[pltpu.VMEM((tm, d), BF16), pltpu.VMEM((tm, 1), F32)],
        compiler_params=_cparams(("parallel", "parallel", "arbitrary")),
        name="s5_gelu_glu",
    )(y2, y2, x, norm_g.reshape(1, d), mod, mod, d_skip.reshape(1, d), w_glu_bf, w_glu_bf,
      b_glu.reshape(1, 2 * d), b_glu.reshape(1, 2 * d), x, mod)


def _router_kernel(x_ref, g_ref, sh_ref, sc_ref, wr_ref, hp_ref, aff_ref, hhi_scr, hlo_scr, inv_scr, *, dh):
    def emit(rows, h):
        hb = h.astype(BF16)
        hbf = hb.astype(F32)
        hhi_scr[rows, :] = hb
        hlo_scr[rows, :] = (h - hbf).astype(BF16)
        bits = lax.bitcast_convert_type(hbf, I32)
        lo = lax.shift_right_logical(bits[:, :dh], 16)
        hi = jnp.bitwise_and(bits[:, dh:], jnp.int32(-65536))
        hp_ref[rows, :] = jnp.bitwise_or(hi, lo)

    _modnorm_rows(x_ref, g_ref, sh_ref, sc_ref, inv_scr, emit)

    w = wr_ref[...]
    w_hi = w.astype(BF16)
    w_lo = (w - w_hi.astype(F32)).astype(BF16)
    nt = (((1,), (1,)), ((), ()))
    ne = w.shape[0]
    both = lax.dot_general(jnp.concatenate([w_hi, w_lo], axis=0), hhi_scr[...], nt, preferred_element_type=F32)
    logits = both[:ne] + both[ne:] + lax.dot_general(w_hi, hlo_scr[...], nt, preferred_element_type=F32)
    m = jnp.max(logits, axis=0, keepdims=True)
    ex = jnp.exp(logits - m)
    aff_ref[...] = ex / jnp.sum(ex, axis=0, keepdims=True)


def _router(x, norm_g, mod, w_router, tm):
    b, l, d = x.shape
    e = w_router.shape[1]
    dh = d // 2
    nt = l // tm
    kern = functools.partial(_router_kernel, dh=dh)
    return pl.pallas_call(
        kern,
        grid=(b, nt),
        in_specs=[
            pl.BlockSpec((1, tm, d), lambda bi, i: (bi, i, 0)),
            pl.BlockSpec((1, d), lambda bi, i: (0, 0)),
            pl.BlockSpec((1, 1, d), lambda bi, i: (bi, 0, 3)),
            pl.BlockSpec((1, 1, d), lambda bi, i: (bi, 0, 4)),
            pl.BlockSpec((e, d), lambda bi, i: (0, 0)),
        ],
        out_specs=[
            pl.BlockSpec((tm, dh), lambda bi, i: (bi * nt + i, 0)),
            pl.BlockSpec((e, tm), lambda bi, i: (0, bi * nt + i)),
        ],
        out_shape=[jax.ShapeDtypeStruct((b * l, dh), I32), jax.ShapeDtypeStruct((e, b * l), F32)],
        scratch_shapes=[pltpu.VMEM((tm, d), BF16), pltpu.VMEM((tm, d), BF16), pltpu.VMEM((tm, 1), F32)],
        compiler_params=_cparams(("parallel", "parallel")),
        name="moe_router",
    )(x, norm_g.reshape(1, d), mod, mod, w_router.T)


def _excl_cumsum_lanes(x):
    n = x.shape[1]
    lane = lax.broadcasted_iota(I32, x.shape, 1)
    inc = x
    k = 1
    while k < n:
        inc = inc + jnp.where(lane >= k, pltpu.roll(inc, k, axis=1), 0)
        k *= 2
    return inc - x


def _select_kernel(aff_ref, idx_ref, gate_ref, off_ref, pos_scr, vals_scr, racc, off_sm, sem, *, cap, ntile, tmtok):
    e, n = aff_ref.shape
    key = lax.bitcast_convert_type(aff_ref[...], I32)

    def bit_step(it, prefix):
        cand = jnp.bitwise_or(prefix, jnp.left_shift(jnp.int32(1), 30 - it))
        cnt = jnp.sum((key >= cand).astype(I32), axis=1, keepdims=True)
        return jnp.where(cnt >= cap, cand, prefix)

    thr = lax.fori_loop(0, 31, bit_step, jnp.zeros((e, 1), I32))
    gt = key > thr
    eq = key == thr
    need = cap - jnp.sum(gt.astype(I32), axis=1, keepdims=True)
    sel = jnp.logical_or(gt, jnp.logical_and(eq, _excl_cumsum_lanes(eq.astype(I32)) < need))
    sel_i = sel.astype(I32)
    pos_scr[...] = jnp.where(sel, _excl_cumsum_lanes(sel_i), -1)

    tok = lax.broadcasted_iota(I32, (e, n), 1)
    lane = lax.broadcasted_iota(I32, off_ref.shape, 1)
    off = jnp.zeros(off_ref.shape, I32)
    for b in range(1, ntile + 1):
        cnt = jnp.sum(jnp.where(tok < b * tmtok, sel_i, 0), axis=1, keepdims=True)
        off = jnp.where(lane == b, cnt, off)
    off_ref[...] = off
    off_copy = pltpu.make_async_copy(off_ref, off_sm, sem)
    off_copy.start()

    tok1 = lax.broadcasted_iota(I32, vals_scr.shape, 1)
    row = lax.broadcasted_iota(I32, vals_scr.shape, 0)
    tok_hi = jnp.right_shift(tok1, 7).astype(F32)
    tok_lo = jnp.bitwise_and(tok1, 127).astype(F32)
    off_copy.wait()

    def expert(ei, _):
        g = aff_ref[pl.ds(ei, 1), :]
        g0 = g.astype(BF16).astype(F32)
        r1 = g - g0
        g1 = r1.astype(BF16).astype(F32)
        g2 = r1 - g1
        vals_scr[...] = jnp.where(row == 0, tok_hi, jnp.where(row == 1, tok_lo, jnp.where(
            row == 2, g0, jnp.where(row == 3, g1, jnp.where(row == 4, g2, 0.0))))).astype(BF16)

        def chunk(ci, _):
            s0 = ci * SLOT_BLOCK
            slot = s0 + lax.broadcasted_iota(I32, (SLOT_BLOCK, tmtok), 0)
            racc[...] = jnp.zeros_like(racc)
            for tb in range(ntile):
                @pl.when(jnp.logical_and(off_sm[ei, tb] < s0 + SLOT_BLOCK, off_sm[ei, tb + 1] > s0))
                def _():
                    cols = pl.ds(tb * tmtok, tmtok)
                    onehot = jnp.where(pos_scr[pl.ds(ei, 1), cols] == slot, 1.0, 0.0).astype(BF16)
                    racc[...] += lax.dot_general(vals_scr[:, cols], onehot, (((1,), (1,)), ((), ())),
                                                 preferred_element_type=F32)
            r = racc[...]
            idx_ref[ei, pl.ds(ci, 1), :] = (r[0:1] * 128.0 + r[1:2]).astype(I32)
            gate_ref[ei, pl.ds(ci, 1), :] = r[2:3] + r[3:4] + r[4:5]
            return 0

        lax.fori_loop(0, cap // SLOT_BLOCK, chunk, 0)
        return 0

    lax.fori_loop(0, e, expert, 0)


def _select(aff_t, cap, ntile, tmtok):
    e, n = aff_t.shape
    nc = cap // SLOT_BLOCK
    kern = functools.partial(_select_kernel, cap=cap, ntile=ntile, tmtok=tmtok)
    idx, gate, off = pl.pallas_call(
        kern,
        out_shape=[jax.ShapeDtypeStruct((e, nc, SLOT_BLOCK), I32),
                   jax.ShapeDtypeStruct((e, nc, SLOT_BLOCK), F32),
                   jax.ShapeDtypeStruct((e, LANES), I32)],
        scratch_shapes=[pltpu.VMEM((e, n), I32), pltpu.VMEM((16, n), BF16), pltpu.VMEM((16, SLOT_BLOCK), F32),
                        pltpu.SMEM((e, LANES), I32), pltpu.SemaphoreType.DMA(())],
        compiler_params=pltpu.CompilerParams(vmem_limit_bytes=VMEM_LIMIT),
        name="moe_select",
    )(aff_t)
    return idx.reshape(e, cap), gate.reshape(e, cap), off


def _ffn_kernel(idx_sm, *refs, ngrp, cap, nf, tf, dh, rows_up, rows_down):
    hp_hbm = refs[:ngrp]
    wg_ref, wu_ref, wd_ref, o_ref, xp_scr, xs_scr, hid_scr, sem = refs[ngrp:]
    e = pl.program_id(0)
    s = pl.program_id(1)
    ne = pl.num_programs(0)
    m = ngrp * cap

    def issue_gather(expert):
        for grp in range(ngrp):
            base = (grp * ne + expert) * cap

            def body(c, _):
                pltpu.make_async_copy(hp_hbm[grp].at[pl.ds(idx_sm[base + c], 1), :],
                                      xp_scr.at[pl.ds(grp * cap + c, 1), :], sem).start()
                return 0

            lax.fori_loop(0, cap, body, 0, unroll=8)

    def wait_gather():
        pltpu.make_async_copy(hp_hbm[0].at[pl.ds(0, m), :], xp_scr, sem).wait()

    nstep = pl.num_programs(1)

    @pl.when(jnp.logical_and(e == 0, s == 0))
    def _():
        issue_gather(0)

    @pl.when(s == 0)
    def _():
        wait_gather()
        p = xp_scr[...]
        xs_scr[0] = lax.bitcast_convert_type(jnp.left_shift(p, 16), F32).astype(BF16)
        xs_scr[1] = lax.bitcast_convert_type(jnp.bitwise_and(p, jnp.int32(-65536)), F32).astype(BF16)

    nxt = jnp.minimum(e + 1, ne - 1)

    def issue_slice(first, count):
        for grp in range(ngrp):
            base = (grp * ne + nxt) * cap + first
            for c in range(count):
                pltpu.make_async_copy(hp_hbm[grp].at[pl.ds(idx_sm[base + c], 1), :],
                                      xp_scr.at[pl.ds(grp * cap + first + c, 1), :], sem).start()

    @pl.when(s < nf)
    def _():
        issue_slice(s * rows_up, rows_up)
        wg = wg_ref[0, 0].astype(BF16)
        wu = wu_ref[0, 0].astype(BF16)
        xl = xs_scr[0]
        xh = xs_scr[1]
        g = (jnp.dot(xl, wg[:dh], preferred_element_type=F32) + jnp.dot(xh, wg[dh:], preferred_element_type=F32))
        u = (jnp.dot(xl, wu[:dh], preferred_element_type=F32) + jnp.dot(xh, wu[dh:], preferred_element_type=F32))
        hid_scr[s] = (_silu(g) * u).astype(BF16)

    @pl.when(s >= nf)
    def _():
        issue_slice(nf * rows_up + (s - nf) * rows_down, rows_down)
        wd = wd_ref[0, 0].astype(BF16)
        acc = jnp.dot(hid_scr[0], wd[:tf], preferred_element_type=F32)
        for k in range(1, nf):
            acc = acc + jnp.dot(hid_scr[k], wd[k * tf:(k + 1) * tf], preferred_element_type=F32)
        for grp in range(ngrp):
            o_ref[grp, 0] = acc[grp * cap:(grp + 1) * cap]

    @pl.when(jnp.logical_and(e == ne - 1, s == nstep - 1))
    def _():
        wait_gather()


def _ffn(idx_flat, hps, w_gate, w_up, w_down, layer, cap, tf, tn):
    ngrp = len(hps)
    n, dh = hps[0].shape
    _, ne, d, f = w_gate.shape
    nf = f // tf
    nn = d // tn
    m = ngrp * cap
    assert all(hp.shape[0] >= m for hp in hps)
    assert (cap * tf) % (f + d) == 0 and (cap * tn) % (f + d) == 0
    kern = functools.partial(_ffn_kernel, ngrp=ngrp, cap=cap, nf=nf, tf=tf, dh=dh,
                             rows_up=cap * tf // (f + d), rows_down=cap * tn // (f + d))
    grid_spec = pltpu.PrefetchScalarGridSpec(
        num_scalar_prefetch=1,
        grid=(ne, nf + nn),
        in_specs=[pl.BlockSpec(memory_space=pl.ANY)] * ngrp + [
            pl.BlockSpec((1, 1, d, tf), lambda e, s, idx: (layer, e, 0, jnp.minimum(s, nf - 1))),
            pl.BlockSpec((1, 1, d, tf), lambda e, s, idx: (layer, e, 0, jnp.minimum(s, nf - 1))),
            pl.BlockSpec((1, 1, f, tn), lambda e, s, idx: (layer, e, 0, jnp.maximum(s - nf, 0))),
        ],
        out_specs=pl.BlockSpec((ngrp, 1, cap, tn), lambda e, s, idx: (0, e, 0, jnp.maximum(s - nf, 0))),
        scratch_shapes=[
            pltpu.VMEM((m, dh), I32),
            pltpu.VMEM((2, m, dh), BF16),
            pltpu.VMEM((nf, m, tf), BF16),
            pltpu.SemaphoreType.DMA(()),
        ],
    )
    return pl.pallas_call(
        kern,
        grid_spec=grid_spec,
        out_shape=jax.ShapeDtypeStruct((ngrp, ne, cap, d), F32),
        compiler_params=_cparams(("arbitrary", "arbitrary")),
        name="moe_ffn",
    )(idx_flat, *hps, w_gate, w_up, w_down)


def _worklist(off, ne, ntile, nblk):
    bs = SLOT_BLOCK
    grp_items = COMBINE_ITEMS
    w_max = ne * (nblk + ntile - 1) + ntile * grp_items
    w_max = -(-w_max // grp_items) * grp_items
    lo = off[:, :ntile].T.reshape(-1)
    hi = off[:, 1:ntile + 1].T.reshape(-1)
    b0 = jnp.minimum(lo // bs, nblk - 1)
    nb = jnp.where(hi > lo, (hi + bs - 1) // bs - b0, 0)
    e_of = jnp.arange(ntile * ne, dtype=I32) % ne
    nb = jnp.where(e_of == 0, jnp.maximum(nb, 1), nb)
    per_tile = jnp.sum(nb.reshape(ntile, ne), axis=1)
    pad = jnp.repeat((-per_tile) % grp_items, ne)
    nbp = nb + jnp.where(e_of == ne - 1, pad, 0)
    end = jnp.cumsum(nbp)
    total = end[-1]
    w = jnp.arange(w_max, dtype=I32)
    wc = jnp.minimum(w, total - 1)
    pair = jnp.sum((end[None, :] <= wc[:, None]).astype(I32), axis=1)
    table = jnp.stack([end - nbp, b0, lo, hi, nb], axis=1)
    onehot = pair[:, None] == jnp.arange(ntile * ne, dtype=I32)[None, :]
    got = jnp.sum(jnp.where(onehot[:, :, None], table[None], 0), axis=1)
    tile = pair // ne
    exp = pair % ne
    k = wc - got[:, 0]
    blk = jnp.minimum(got[:, 1] + k, nblk - 1)
    valid = jnp.logical_and(w < total, k < got[:, 4])
    rlo = jnp.where(valid, jnp.clip(got[:, 2] - blk * bs, 0, bs), 0)
    rhi = jnp.where(valid, jnp.clip(got[:, 3] - blk * bs, 0, bs), 0)
    first = jnp.concatenate([jnp.ones((1,), bool), tile[1:] != tile[:-1]])
    last = jnp.concatenate([tile[1:] != tile[:-1], jnp.ones((1,), bool)])
    to_i = lambda a: a.astype(I32)
    return to_i(tile), to_i(exp), to_i(blk), to_i(rlo), to_i(rhi), to_i(first), to_i(last)


def _combine_kernel(tile_sm, exp_sm, blk_sm, rlo_sm, rhi_sm, first_sm, last_sm, tq_sm, tu_sm, gate_sm, *refs,
                    cap, final):
    src_refs = refs[:COMBINE_ITEMS]
    x_ref, g2_ref, fg_ref, o_ref, acc_scr = refs[COMBINE_ITEMS:]
    step = pl.program_id(0)

    @pl.when(first_sm[step * COMBINE_ITEMS] == 1)
    def _():
        acc_scr[...] = jnp.zeros_like(acc_scr)

    def item(w, src_ref):
        slot0 = exp_sm[w] * cap + blk_sm[w] * SLOT_BLOCK
        rlo = rlo_sm[w]
        rhi = rhi_sm[w]
        body0 = jnp.minimum(((rlo + COMBINE_ROWS - 1) // COMBINE_ROWS) * COMBINE_ROWS, rhi)
        body1 = jnp.maximum((rhi // COMBINE_ROWS) * COMBINE_ROWS, body0)

        def row(r, _):
            q, u = tq_sm[slot0 + r], tu_sm[slot0 + r]
            acc_scr[q, pl.ds(u, 1), :] = (acc_scr[q, pl.ds(u, 1), :]
                                          + src_ref[0, 0, r // COMBINE_ROWS, pl.ds(r % COMBINE_ROWS, 1), :]
                                          * gate_sm[slot0 + r])
            return 0

        def group(gi, _):
            sq = body0 // COMBINE_ROWS + gi
            s0 = slot0 + sq * COMBINE_ROWS
            qs = [tq_sm[s0 + u] for u in range(COMBINE_ROWS)]
            us = [tu_sm[s0 + u] for u in range(COMBINE_ROWS)]
            new = [acc_scr[qs[u], pl.ds(us[u], 1), :] + src_ref[0, 0, sq, u:u + 1, :] * gate_sm[s0 + u]
                   for u in range(COMBINE_ROWS)]
            for u in range(COMBINE_ROWS):
                acc_scr[qs[u], pl.ds(us[u], 1), :] = new[u]
            return 0

        lax.fori_loop(rlo, body0, row, 0)
        lax.fori_loop(0, (body1 - body0) // COMBINE_ROWS, group, 0)
        lax.fori_loop(body1, rhi, row, 0)

    for j in range(COMBINE_ITEMS):
        item(step * COMBINE_ITEMS + j, src_refs[j])

    @pl.when(last_sm[step * COMBINE_ITEMS + COMBINE_ITEMS - 1] == 1)
    def _():
        xn = x_ref[0] + g2_ref[0] * acc_scr[...]
        if final:
            ms = jnp.mean(xn * xn, axis=-1, keepdims=True)
            xn = xn * lax.rsqrt(ms + RMS_EPS) * fg_ref[...]
        o_ref[0] = xn


def _combine(work, idx_flat_g, gate_flat_g, outs, grp, x, mod, final_g, tmtok, final):
    b, l, d = x.shape
    ngrp, ne, cap = outs.shape[:3]
    tpb = l // tmtok
    w_max = work[0].shape[0]
    rs = COMBINE_ROWS
    kern = functools.partial(_combine_kernel, cap=cap, final=final)
    tloc = idx_flat_g % tmtok
    outs_r = outs.reshape(ngrp, ne, cap // rs, rs, d)
    x_r = x.reshape(b, l // rs, rs, d)
    tile_rows = tmtok // rs
    gi = COMBINE_ITEMS

    def src_spec(j):
        return pl.BlockSpec((1, 1, SLOT_BLOCK // rs, rs, d),
                            lambda w, ti, ex, bl, *_: (grp, ex[w * gi + j], bl[w * gi + j], 0, 0))

    def tile_of(w, ti):
        t = ti[w * gi]
        return t // tpb, t % tpb

    grid_spec = pltpu.PrefetchScalarGridSpec(
        num_scalar_prefetch=10,
        grid=(w_max // gi,),
        in_specs=[src_spec(j) for j in range(gi)] + [
            pl.BlockSpec((1, tile_rows, rs, d), lambda w, ti, *_: tile_of(w, ti) + (0, 0),
                         pipeline_mode=pl.Buffered(1)),
            pl.BlockSpec((1, 1, d), lambda w, ti, *_: (tile_of(w, ti)[0], 0, 5)),
            pl.BlockSpec((1, d), lambda w, *_: (0, 0)),
        ],
        out_specs=pl.BlockSpec((1, tile_rows, rs, d), lambda w, ti, *_: tile_of(w, ti) + (0, 0)),
        scratch_shapes=[pltpu.VMEM((tile_rows, rs, d), F32)],
    )
    out = pl.pallas_call(
        kern,
        grid_spec=grid_spec,
        out_shape=jax.ShapeDtypeStruct((b, l // rs, rs, d), F32),
        compiler_params=_cparams(("arbitrary",)),
        name="moe_combine",
    )(*work, tloc // rs, tloc % rs, gate_flat_g, *([outs_r] * gi), x_r, mod, final_g.reshape(1, d))
    return out.reshape(b, l, d)


def _pick(n, pref):
    t = min(pref, n)
    while n % t:
        t //= 2
    return t


def _moe_layer(xs, mods, norm_g, w_router, w_gate, w_up, w_down, layer, final_g, final):
    ne = w_router.shape[1]
    hps, idxs, gates, works = [], [], [], []
    caps = set()
    for x, mod in zip(xs, mods):
        b, l, d = x.shape
        n = b * l
        cap = EC_CAPACITY_FACTOR * n // ne
        caps.add(cap)
        tmtok = _pick(l, 1024)
        ntile = n // tmtok
        hp, aff_t = _router(x, norm_g, mod, w_router, _pick(l, 512))
        idx, gate, off = _select(aff_t, cap, ntile, tmtok)
        hps.append(hp)
        idxs.append(idx.reshape(-1))
        gates.append(gate.reshape(-1))
        works.append(_worklist(off, ne, ntile, cap // SLOT_BLOCK))
    assert len(caps) == 1, "request groups must have equal expert capacity"
    cap = caps.pop()
    f = w_gate.shape[3]
    outs = _ffn(jnp.concatenate(idxs), hps, w_gate, w_up, w_down, layer, cap,
                _pick(f, 256), _pick(xs[0].shape[2], 256))
    new = []
    for grp, (x, mod) in enumerate(zip(xs, mods)):
        tmtok = _pick(x.shape[1], 1024)
        new.append(_combine(works[grp], idxs[grp], gates[grp], outs, grp, x, mod, final_g, tmtok, final))
    return new


def kernel(x_prompt, x_sample, c_prompt, c_sample, ada_w, ada_b, norm_mix_g, norm_ffn_g, final_norm_g, conv_w_in, conv_b_in, conv_w_dw, conv_b_dw, conv_ln_g, conv_ln_b, conv_w_out, conv_b_out, ssm_lambda_re, ssm_lambda_im, ssm_log_step, ssm_b_re, ssm_b_im, ssm_c_re, ssm_c_im, ssm_d, ssm_w_glu, ssm_b_glu, moe_w_router, moe_w_gate, moe_w_up, moe_w_down):
    depth = ada_w.shape[0]
    d = x_prompt.shape[-1]
    bp, bs = x_prompt.shape[0], x_sample.shape[0]
    nrow = 16 * ((bp + bs + 15) // 16)
    c_all = jnp.concatenate([c_prompt, c_sample, jnp.zeros((nrow - bp - bs, d), F32)], axis=0)
    mod_all = _ada(c_all, ada_w, ada_b)

    xs = [x_prompt, x_sample]
    for i in range(depth):
        mods = [mod_all[i, :bp].reshape(bp, 1, 6 * d), mod_all[i, bp:bp + bs].reshape(bs, 1, 6 * d)]
        j = i // 2
        if i % 2 == 0:
            w_in = conv_w_in[j].astype(BF16)
            w_out = conv_w_out[j].astype(BF16)
            nxt = []
            for x, mod in zip(xs, mods):
                l = x.shape[1]
                u = _conv_in(x, norm_mix_g[i], mod, w_in, conv_b_in[j], _pick(l, 512), _pick(d, 1024))
                nxt.append(_conv_out(u, x, mod, conv_w_dw[j], conv_b_dw[j], conv_ln_g[j], conv_ln_b[j],
                                     w_out, conv_b_out[j], _pick(l, 512), _pick(d, 1024)))
            xs = nxt
        else:
            lay = _s5_layout(ssm_lambda_re[j], ssm_lambda_im[j], ssm_log_step[j], ssm_b_re[j], ssm_b_im[j],
                             ssm_c_re[j], ssm_c_im[j])
            w_glu = ssm_w_glu[j].astype(BF16)
            nxt = []
            for x, mod in zip(xs, mods):
                l = x.shape[1]
                y2 = _s5_scan(x, norm_mix_g[i], mod, lay, _pick(l, 256))
                nxt.append(_s5_glu(y2, x, norm_mix_g[i], mod, ssm_d[j], w_glu, ssm_b_glu[j],
                                   _pick(l, 512), _pick(d, 1024)))
            xs = nxt
        xs = _moe_layer(xs, mods, norm_ffn_g[i], moe_w_router[i], moe_w_gate, moe_w_up, moe_w_down, i,
                        final_norm_g, final=(i == depth - 1))
    return (xs[0], xs[1])
```

```python
import functools
import math

import jax
import jax.numpy as jnp
from jax import lax
from jax.experimental import pallas as pl
from jax.experimental.pallas import tpu as pltpu

F32 = jnp.float32
BF16 = jnp.bfloat16
I32 = jnp.int32
HIGHEST = lax.Precision.HIGHEST

RMS_EPS = 1e-6
LN_EPS = 1e-5
CONV_WIDTH = 31
CONV_PAD = (CONV_WIDTH - 1) // 2
CONV_HALO = 16
SSM_GROUP = 16
SSM_STATE = 64
SSM_BLOCKS = 8
S5_LAG = 4
N_EXPERTS = 16
EC_CAPACITY_FACTOR = 2
LANES = 128
SLOT_BLOCK = 128
COMBINE_ROWS = 8
COMBINE_ITEMS = 6
ROW_CHUNK = 16

VMEM_LIMIT = 56 * 1024 * 1024


def _cparams(sem, vmem=VMEM_LIMIT):
    return pltpu.CompilerParams(dimension_semantics=sem, vmem_limit_bytes=vmem)


def _modnorm_rows(x_ref, g_ref, sh_ref, sc_ref, inv_scr, emit):
    x = x_ref[0]
    inv_scr[...] = lax.rsqrt(jnp.mean(x * x, axis=-1, keepdims=True) + RMS_EPS)
    gain = g_ref[...] * (1.0 + sc_ref[0])
    shift = sh_ref[0]

    def chunk(r, _):
        rows = pl.ds(pl.multiple_of(r * ROW_CHUNK, ROW_CHUNK), ROW_CHUNK)
        emit(rows, x_ref[0, rows, :] * inv_scr[rows, :] * gain + shift)
        return 0

    lax.fori_loop(0, x.shape[0] // ROW_CHUNK, chunk, 0, unroll=2)


def _silu(x):
    return x * jax.nn.sigmoid(x)


def _gelu_tanh(x):
    c = math.sqrt(2.0 / math.pi)
    return 0.5 * x * (1.0 + jnp.tanh(c * (x + 0.044715 * (x * x * x))))


def _ada_kernel(c_ref, w_ref, b_ref, o_ref):
    c = c_ref[...]
    cond = _silu(c)
    c_hi = cond.astype(BF16)
    c_lo = (cond - c_hi.astype(F32)).astype(BF16)
    w = w_ref[0]
    w_hi = w.astype(BF16)
    w_lo = (w - w_hi.astype(F32)).astype(BF16)
    o_ref[0] = (jnp.dot(c_hi, w_hi, preferred_element_type=F32) + jnp.dot(c_lo, w_hi, preferred_element_type=F32)
                + jnp.dot(c_hi, w_lo, preferred_element_type=F32) + b_ref[0])


def _ada(c_all, ada_w, ada_b):
    depth, d, n6 = ada_w.shape
    rows = c_all.shape[0]
    tn = _pick(n6, 1024)
    return pl.pallas_call(
        _ada_kernel,
        grid=(depth, n6 // tn),
        in_specs=[
            pl.BlockSpec((rows, d), lambda i, j: (0, 0)),
            pl.BlockSpec((1, d, tn), lambda i, j: (i, 0, j)),
            pl.BlockSpec((1, 1, tn), lambda i, j: (i, 0, j)),
        ],
        out_specs=pl.BlockSpec((1, rows, tn), lambda i, j: (i, 0, j)),
        out_shape=jax.ShapeDtypeStruct((depth, rows, n6), F32),
        compiler_params=_cparams(("parallel", "parallel")),
        name="ada_mod",
    )(c_all, ada_w, ada_b.reshape(depth, 1, n6))


def _conv_in_kernel(x_ref, g_ref, sh_ref, sc_ref, wa_ref, wg_ref, ba_ref, bg_ref, o_ref, h_scr, inv_scr):
    @pl.when(pl.program_id(2) == 0)
    def _():
        def emit(rows, h):
            h_scr[rows, :] = h.astype(BF16)

        _modnorm_rows(x_ref, g_ref, sh_ref, sc_ref, inv_scr, emit)

    h = h_scr[...]
    a = jnp.dot(h, wa_ref[...], preferred_element_type=F32) + ba_ref[...]
    g = jnp.dot(h, wg_ref[...], preferred_element_type=F32) + bg_ref[...]
    o_ref[0] = a * jax.nn.sigmoid(g)


def _conv_in(x, norm_g, mod, w_in_bf, b_in, tm, tn):
    b, l, d = x.shape
    nj = d // tn
    return pl.pallas_call(
        _conv_in_kernel,
        grid=(b, l // tm, nj),
        in_specs=[
            pl.BlockSpec((1, tm, d), lambda bi, i, j: (bi, i, 0)),
            pl.BlockSpec((1, d), lambda bi, i, j: (0, 0)),
            pl.BlockSpec((1, 1, d), lambda bi, i, j: (bi, 0, 0)),
            pl.BlockSpec((1, 1, d), lambda bi, i, j: (bi, 0, 1)),
            pl.BlockSpec((d, tn), lambda bi, i, j: (0, j)),
            pl.BlockSpec((d, tn), lambda bi, i, j: (0, j + nj)),
            pl.BlockSpec((1, tn), lambda bi, i, j: (0, j)),
            pl.BlockSpec((1, tn), lambda bi, i, j: (0, j + nj)),
        ],
        out_specs=pl.BlockSpec((1, tm, tn), lambda bi, i, j: (bi, i, j)),
        out_shape=jax.ShapeDtypeStruct((b, l, d), F32),
        scratch_shapes=[pltpu.VMEM((tm, d), BF16), pltpu.VMEM((tm, 1), F32)],
        compiler_params=_cparams(("parallel", "parallel", "arbitrary")),
        name="conv_in_glu",
    )(x, norm_g.reshape(1, d), mod, mod, w_in_bf, w_in_bf, b_in.reshape(1, 2 * d), b_in.reshape(1, 2 * d))


def _conv_out_kernel(u_ref, up_ref, un_ref, wdw_ref, bdw_ref, lg_ref, lb_ref, wo_ref, bo_ref, x_ref, g1_ref,
                     o_ref, ext_scr, c_scr, v_scr, mu_scr, inv_scr, *, tm, d, rc, cc):
    i = pl.program_id(1)
    ni = pl.num_programs(1)

    ncol = d // cc

    @pl.when(pl.program_id(2) == 0)
    def _():
        for c in range(ncol):
            cols = pl.ds(c * cc, cc)
            ext_scr[c, pl.ds(0, CONV_HALO), :] = jnp.where(i > 0, up_ref[0, :, cols], 0.0)
            ext_scr[c, pl.ds(CONV_HALO, tm), :] = u_ref[0, :, cols]
            ext_scr[c, pl.ds(CONV_HALO + tm, CONV_HALO), :] = jnp.where(i < ni - 1, un_ref[0, :, cols], 0.0)

        def col_chunk(c, _):
            for r0 in range(0, tm, rc):
                acc = jnp.zeros((rc, cc), F32) + bdw_ref[c]
                for k in range(CONV_WIDTH):
                    seg = ext_scr.at[c][pl.ds(r0 + (CONV_HALO - CONV_PAD + k), rc, stride=1), :]
                    acc = acc + seg * wdw_ref[c, pl.ds(k, 1), :]
                c_scr[c, pl.ds(r0, rc), :] = acc
            return 0

        lax.fori_loop(0, ncol, col_chunk, 0)

        def rows_of(r):
            return pl.ds(pl.multiple_of(r * ROW_CHUNK, ROW_CHUNK), ROW_CHUNK)

        def mean_chunk(r, _):
            rows = rows_of(r)
            tot = functools.reduce(lambda a, v: a + v, [c_scr[c, rows, :] for c in range(ncol)])
            mu_scr[rows, :] = jnp.sum(tot, axis=-1, keepdims=True) * (1.0 / d)
            return 0

        def var_chunk(r, _):
            rows = rows_of(r)
            mu = mu_scr[rows, :]
            dev = [c_scr[c, rows, :] - mu for c in range(ncol)]
            tot = functools.reduce(lambda a, v: a + v, [v * v for v in dev])
            inv_scr[rows, :] = lax.rsqrt(jnp.sum(tot, axis=-1, keepdims=True) * (1.0 / d) + LN_EPS)
            return 0

        def norm_chunk(r, _):
            rows = rows_of(r)
            mu = mu_scr[rows, :]
            inv = inv_scr[rows, :]
            for c in range(ncol):
                cols = pl.ds(c * cc, cc)
                y = (c_scr[c, rows, :] - mu) * inv * lg_ref[:, cols] + lb_ref[:, cols]
                v_scr[rows, cols] = _silu(y).astype(BF16)
            return 0

        for body, unroll in ((mean_chunk, 16), (var_chunk, 4), (norm_chunk, 4)):
            lax.fori_loop(0, tm // ROW_CHUNK, body, 0, unroll=min(unroll, tm // ROW_CHUNK))

    o = jnp.dot(v_scr[...], wo_ref[...], preferred_element_type=F32) + bo_ref[...]
    o_ref[0] = x_ref[0] + g1_ref[0] * o


def _conv_out(u, x, mod, w_dw, b_dw, ln_g, ln_b, w_out_bf, b_out, tm, tn):
    b, l, d = x.shape
    nj = d // tn
    hb = tm // CONV_HALO
    nhb = l // CONV_HALO
    rc = min(64, tm)
    cc = min(LANES, d)
    ncol = d // cc
    w_dw_p = jnp.concatenate([w_dw, jnp.zeros((1, d), w_dw.dtype)], axis=0)
    w_dw_p = w_dw_p.reshape(CONV_WIDTH + 1, ncol, cc).transpose(1, 0, 2)
    kern = functools.partial(_conv_out_kernel, tm=tm, d=d, rc=rc, cc=cc)
    return pl.pallas_call(
        kern,
        grid=(b, l // tm, nj),
        in_specs=[
            pl.BlockSpec((1, tm, d), lambda bi, i, j: (bi, i, 0)),
            pl.BlockSpec((1, CONV_HALO, d), lambda bi, i, j: (bi, jnp.maximum(i * hb - 1, 0), 0)),
            pl.BlockSpec((1, CONV_HALO, d), lambda bi, i, j: (bi, jnp.minimum((i + 1) * hb, nhb - 1), 0)),
            pl.BlockSpec((ncol, CONV_WIDTH + 1, cc), lambda bi, i, j: (0, 0, 0)),
            pl.BlockSpec((ncol, 1, cc), lambda bi, i, j: (0, 0, 0)),
            pl.BlockSpec((1, d), lambda bi, i, j: (0, 0)),
            pl.BlockSpec((1, d), lambda bi, i, j: (0, 0)),
            pl.BlockSpec((d, tn), lambda bi, i, j: (0, j)),
            pl.BlockSpec((1, tn), lambda bi, i, j: (0, j)),
            pl.BlockSpec((1, tm, tn), lambda bi, i, j: (bi, i, j)),
            pl.BlockSpec((1, 1, tn), lambda bi, i, j: (bi, 0, 2 * nj + j)),
        ],
        out_specs=pl.BlockSpec((1, tm, tn), lambda bi, i, j: (bi, i, j)),
        out_shape=jax.ShapeDtypeStruct((b, l, d), F32),
        scratch_shapes=[
            pltpu.VMEM((ncol, tm + 2 * CONV_HALO, cc), F32),
            pltpu.VMEM((ncol, tm, cc), F32),
            pltpu.VMEM((tm, d), BF16),
            pltpu.VMEM((tm, 1), F32),
            pltpu.VMEM((tm, 1), F32),
        ],
        compiler_params=_cparams(("parallel", "parallel", "arbitrary")),
        name="conv_dw_ln_out",
    )(u, u, u, w_dw_p, b_dw.reshape(ncol, 1, cc), ln_g.reshape(1, d), ln_b.reshape(1, d), w_out_bf,
      b_out.reshape(1, d), x, mod)


def _zoh_discretise(lam_re, lam_im, log_step, b_re, b_im):
    dt = jnp.exp(log_step)[..., None]
    mag = jnp.exp(lam_re * dt)
    a_re = mag * jnp.cos(lam_im * dt)
    a_im = mag * jnp.sin(lam_im * dt)
    nr = a_re - 1.0
    ni = a_im
    den = lam_re * lam_re + lam_im * lam_im
    k_re = (nr * lam_re + ni * lam_im) / den
    k_im = (ni * lam_re - nr * lam_im) / den
    bb_re = k_re[..., None] * b_re - k_im[..., None] * b_im
    bb_im = k_re[..., None] * b_im + k_im[..., None] * b_re
    return a_re, a_im, bb_re, bb_im


def _s5_layout(lam_re, lam_im, log_step, b_re, b_im, c_re, c_im):
    a_re, a_im, bb_re, bb_im = _zoh_discretise(lam_re, lam_im, log_step, b_re, b_im)
    nd, g, p = a_re.shape
    c = b_re.shape[-1]
    gpb = g // SSM_BLOCKS
    chb, nsb = gpb * c, gpb * p

    def rows_gc(bb):
        return bb.reshape(nd, SSM_BLOCKS, gpb, p, c).transpose(0, 1, 2, 4, 3).reshape(nd, SSM_BLOCKS, chb, p)

    def rows_gp(cc):
        return cc.reshape(nd, SSM_BLOCKS, gpb, c, p).transpose(0, 1, 2, 4, 3).reshape(nd, SSM_BLOCKS, nsb, c)

    def kern(er_ref, ei_ref, cr_ref, ci_ref, wer_ref, wei_ref, wcr_ref, wci_ref):
        def spread(x, width, rows_per_group, cols_per_group):
            k = x.shape[1]
            rep = (lax.broadcasted_iota(I32, (k, width), 1) % k == lax.broadcasted_iota(I32, (k, width), 0))
            full = jnp.dot(x.astype(BF16), rep.astype(BF16), preferred_element_type=F32)
            rg = lax.broadcasted_iota(I32, full.shape, 0) // rows_per_group
            cg = lax.broadcasted_iota(I32, full.shape, 1) // cols_per_group
            return jnp.where(rg == cg, full, 0.0).astype(BF16)

        wer_ref[0, 0] = spread(er_ref[0, 0], nsb, c, p)
        wei_ref[0, 0] = spread(ei_ref[0, 0], nsb, c, p)
        wcr_ref[0, 0] = spread(cr_ref[0, 0], chb, p, c)
        wci_ref[0, 0] = spread(ci_ref[0, 0], chb, p, c)

    blk = lambda r, k: pl.BlockSpec((1, 1, r, k), lambda dd, s: (dd, s, 0, 0))
    we_re, we_im, wc_re, wc_im = pl.pallas_call(
        kern,
        grid=(nd, SSM_BLOCKS),
        in_specs=[blk(chb, p), blk(chb, p), blk(nsb, c), blk(nsb, c)],
        out_specs=[blk(chb, nsb), blk(chb, nsb), blk(nsb, chb), blk(nsb, chb)],
        out_shape=[jax.ShapeDtypeStruct((nd, SSM_BLOCKS, chb, nsb), BF16)] * 2
        + [jax.ShapeDtypeStruct((nd, SSM_BLOCKS, nsb, chb), BF16)] * 2,
        compiler_params=_cparams(("parallel", "parallel")),
        name="s5_weight_layout",
    )(rows_gc(bb_re), rows_gc(bb_im), rows_gp(c_re), rows_gp(c_im))
    return (a_re.reshape(nd, SSM_BLOCKS, nsb), a_im.reshape(nd, SSM_BLOCKS, nsb), we_re, we_im, wc_re, wc_im)


def _s5_kernel(x_ref, g_ref, sh_ref, sc_ref, ar_ref, ai_ref, wer_ref, wei_ref, wcr_ref, wci_ref,
               o_ref, sre, sim, cre, cim, h_scr, inv_scr, *, t, chb, nsb, pitch):
    dirn = pl.program_id(0)
    nslab = nsb // LANES

    @pl.when(jnp.logical_and(jnp.logical_and(dirn == 0, pl.program_id(1) == 0), pl.program_id(2) == 0))
    def _():
        sre[...] = jnp.zeros_like(sre)
        sim[...] = jnp.zeros_like(sim)

    @pl.when(pl.program_id(2) == 0)
    def _():
        cre[...] = jnp.zeros_like(cre)
        cim[...] = jnp.zeros_like(cim)

    def emit(rows, h):
        h_scr[rows, :] = h.astype(BF16)

    _modnorm_rows(x_ref, g_ref, sh_ref, sc_ref, inv_scr, emit)

    def base(s):
        return s * pitch + (S5_LAG if s % 2 else 0)

    for s in range(SSM_BLOCKS):
        hs = h_scr[:, s * chb:(s + 1) * chb]
        bur = jnp.dot(hs, wer_ref[0, s], preferred_element_type=F32)
        bui = jnp.dot(hs, wei_ref[0, s], preferred_element_type=F32)
        for j in range(nslab):
            sre[j, pl.ds(base(s), t), :] = bur[:, j * LANES:(j + 1) * LANES]
            sim[j, pl.ds(base(s), t), :] = bui[:, j * LANES:(j + 1) * LANES]

    ar = [ar_ref[0, :, j * LANES:(j + 1) * LANES] for j in range(nslab)]
    ai = [ai_ref[0, :, j * LANES:(j + 1) * LANES] for j in range(nslab)]
    odd = jnp.bitwise_and(lax.broadcasted_iota(I32, (SSM_BLOCKS, LANES), 0), 1)
    nrow = t + S5_LAG

    def make_step(active):
        def step(k, carry):
            xr, xi = carry
            row = jnp.where(dirn == 0, k, nrow - 1 - k)
            rows = pl.ds(row, SSM_BLOCKS, stride=pitch)
            nr, ni = [], []
            for j in range(nslab):
                br = sre.at[j][rows, :]
                bi = sim.at[j][rows, :]
                r = ar[j] * xr[j] - ai[j] * xi[j] + br
                im = ar[j] * xi[j] + ai[j] * xr[j] + bi
                sre.at[j][rows, :] = r
                sim.at[j][rows, :] = im
                if active is not None:
                    r = jnp.where(active, r, xr[j])
                    im = jnp.where(active, im, xi[j])
                nr.append(r)
                ni.append(im)
            return tuple(nr), tuple(ni)
        return step

    x0 = (tuple(cre[:, j * LANES:(j + 1) * LANES] for j in range(nslab)),
          tuple(cim[:, j * LANES:(j + 1) * LANES] for j in range(nslab)))
    x1 = lax.fori_loop(0, S5_LAG, make_step(odd == dirn), x0)
    x2 = lax.fori_loop(S5_LAG, t, make_step(None), x1, unroll=4)
    xr, xi = lax.fori_loop(t, nrow, make_step(odd != dirn), x2)
    for j in range(nslab):
        cre[:, j * LANES:(j + 1) * LANES] = xr[j]
        cim[:, j * LANES:(j + 1) * LANES] = xi[j]

    for s in range(SSM_BLOCKS):
        sr = jnp.concatenate([sre[j, pl.ds(base(s), t), :] for j in range(nslab)], axis=1).astype(BF16)
        si = jnp.concatenate([sim[j, pl.ds(base(s), t), :] for j in range(nslab)], axis=1).astype(BF16)
        y = (jnp.dot(sr, wcr_ref[0, s], preferred_element_type=F32)
             - jnp.dot(si, wci_ref[0, s], preferred_element_type=F32))
        o_ref[0, 0, :, s * chb:(s + 1) * chb] = y


def _s5_scan(x, norm_g, mod, lay, t):
    a_re, a_im, we_re, we_im, wc_re, wc_im = lay
    b, l, d = x.shape
    nd = a_re.shape[0]
    chb = d // SSM_BLOCKS
    nsb = a_re.shape[-1]
    nslab = nsb // LANES
    nt = l // t
    pitch = t + S5_LAG
    kern = functools.partial(_s5_kernel, t=t, chb=chb, nsb=nsb, pitch=pitch)

    def tile(dd, i):
        return i + dd * (nt - 1 - 2 * i)

    single = pl.Buffered(1)
    return pl.pallas_call(
        kern,
        grid=(nd, b, nt),
        in_specs=[
            pl.BlockSpec((1, t, d), lambda dd, bi, i: (bi, tile(dd, i), 0)),
            pl.BlockSpec((1, d), lambda dd, bi, i: (0, 0)),
            pl.BlockSpec((1, 1, d), lambda dd, bi, i: (bi, 0, 0)),
            pl.BlockSpec((1, 1, d), lambda dd, bi, i: (bi, 0, 1)),
            pl.BlockSpec((1, SSM_BLOCKS, nsb), lambda dd, bi, i: (dd, 0, 0)),
            pl.BlockSpec((1, SSM_BLOCKS, nsb), lambda dd, bi, i: (dd, 0, 0)),
            pl.BlockSpec((1, SSM_BLOCKS, chb, nsb), lambda dd, bi, i: (dd, 0, 0, 0), pipeline_mode=single),
            pl.BlockSpec((1, SSM_BLOCKS, chb, nsb), lambda dd, bi, i: (dd, 0, 0, 0), pipeline_mode=single),
            pl.BlockSpec((1, SSM_BLOCKS, nsb, chb), lambda dd, bi, i: (dd, 0, 0, 0), pipeline_mode=single),
            pl.BlockSpec((1, SSM_BLOCKS, nsb, chb), lambda dd, bi, i: (dd, 0, 0, 0), pipeline_mode=single),
        ],
        out_specs=pl.BlockSpec((1, 1, t, d), lambda dd, bi, i: (dd, bi, tile(dd, i), 0)),
        out_shape=jax.ShapeDtypeStruct((nd, b, l, d), F32),
        scratch_shapes=[
            pltpu.VMEM((nslab, SSM_BLOCKS * pitch, LANES), F32),
            pltpu.VMEM((nslab, SSM_BLOCKS * pitch, LANES), F32),
            pltpu.VMEM((SSM_BLOCKS, nsb), F32),
            pltpu.VMEM((SSM_BLOCKS, nsb), F32),
            pltpu.VMEM((t, d), BF16),
            pltpu.VMEM((t, 1), F32),
        ],
        compiler_params=_cparams(("arbitrary", "arbitrary", "arbitrary")),
        name="s5_scan",
    )(x, norm_g.reshape(1, d), mod, mod, a_re, a_im, we_re, we_im, wc_re, wc_im)


def _s5_glu_kernel(yf_ref, yb_ref, xf_ref, g_ref, sh_ref, sc_ref, dsk_ref, wa_ref, wg_ref, ba_ref, bg_ref,
                   x_ref, g1_ref, o_ref, v_scr, inv_scr):
    @pl.when(pl.program_id(2) == 0)
    def _():
        def emit(rows, h):
            y = yf_ref[0, 0, rows, :] + yb_ref[0, 0, rows, :] + dsk_ref[...] * h
            v_scr[rows, :] = _gelu_tanh(y).astype(BF16)

        _modnorm_rows(xf_ref, g_ref, sh_ref, sc_ref, inv_scr, emit)

    v = v_scr[...]
    a = jnp.dot(v, wa_ref[...], preferred_element_type=F32) + ba_ref[...]
    g = jnp.dot(v, wg_ref[...], preferred_element_type=F32) + bg_ref[...]
    o_ref[0] = x_ref[0] + g1_ref[0] * (a * jax.nn.sigmoid(g))


def _s5_glu(y2, x, norm_g, mod, d_skip, w_glu_bf, b_glu, tm, tn):
    b, l, d = x.shape
    nj = d // tn
    return pl.pallas_call(
        _s5_glu_kernel,
        grid=(b, l // tm, nj),
        in_specs=[
            pl.BlockSpec((1, 1, tm, d), lambda bi, i, j: (0, bi, i, 0)),
            pl.BlockSpec((1, 1, tm, d), lambda bi, i, j: (1, bi, i, 0)),
            pl.BlockSpec((1, tm, d), lambda bi, i, j: (bi, i, 0)),
            pl.BlockSpec((1, d), lambda bi, i, j: (0, 0)),
            pl.BlockSpec((1, 1, d), lambda bi, i, j: (bi, 0, 0)),
            pl.BlockSpec((1, 1, d), lambda bi, i, j: (bi, 0, 1)),
            pl.BlockSpec((1, d), lambda bi, i, j: (0, 0)),
            pl.BlockSpec((d, tn), lambda bi, i, j: (0, j)),
            pl.BlockSpec((d, tn), lambda bi, i, j: (0, j + nj)),
            pl.BlockSpec((1, tn), lambda bi, i, j: (0, j)),
            pl.BlockSpec((1, tn), lambda bi, i, j: (0, j + nj)),
            pl.BlockSpec((1, tm, tn), lambda bi, i, j: (bi, i, j)),
            pl.BlockSpec((1, 1, tn), lambda bi, i, j: (bi, 0, 2 * nj + j)),
        ],
        out_specs=pl.BlockSpec((1, tm, tn), lambda bi, i, j: (bi, i, j)),
        out_shape=jax.ShapeDtypeStruct((b, l, d), F32),
        scratch_shapes=[pltpu.VMEM((tm, d), BF16), pltpu.VMEM((tm, 1), F32)],
        compiler_params=_cparams(("parallel", "parallel", "arbitrary")),
        name="s5_gelu_glu",
    )(y2, y2, x, norm_g.reshape(1, d), mod, mod, d_skip.reshape(1, d), w_glu_bf, w_glu_bf,
      b_glu.reshape(1, 2 * d), b_glu.reshape(1, 2 * d), x, mod)


def _router_kernel(x_ref, g_ref, sh_ref, sc_ref, wr_ref, hp_ref, aff_ref, hhi_scr, hlo_scr, inv_scr, *, dh):
    def emit(rows, h):
        hb = h.astype(BF16)
        hbf = hb.astype(F32)
        hhi_scr[rows, :] = hb
        hlo_scr[rows, :] = (h - hbf).astype(BF16)
        bits = lax.bitcast_convert_type(hbf, I32)
        lo = lax.shift_right_logical(bits[:, :dh], 16)
        hi = jnp.bitwise_and(bits[:, dh:], jnp.int32(-65536))
        hp_ref[rows, :] = jnp.bitwise_or(hi, lo)

    _modnorm_rows(x_ref, g_ref, sh_ref, sc_ref, inv_scr, emit)

    w = wr_ref[...]
    w_hi = w.astype(BF16)
    w_lo = (w - w_hi.astype(F32)).astype(BF16)
    nt = (((1,), (1,)), ((), ()))
    ne = w.shape[0]
    both = lax.dot_general(jnp.concatenate([w_hi, w_lo], axis=0), hhi_scr[...], nt, preferred_element_type=F32)
    logits = both[:ne] + both[ne:] + lax.dot_general(w_hi, hlo_scr[...], nt, preferred_element_type=F32)
    m = jnp.max(logits, axis=0, keepdims=True)
    ex = jnp.exp(logits - m)
    aff_ref[...] = ex / jnp.sum(ex, axis=0, keepdims=True)


def _router(x, norm_g, mod, w_router, tm):
    b, l, d = x.shape
    e = w_router.shape[1]
    dh = d // 2
    nt = l // tm
    kern = functools.partial(_router_kernel, dh=dh)
    return pl.pallas_call(
        kern,
        grid=(b, nt),
        in_specs=[
            pl.BlockSpec((1, tm, d), lambda bi, i: (bi, i, 0)),
            pl.BlockSpec((1, d), lambda bi, i: (0, 0)),
            pl.BlockSpec((1, 1, d), lambda bi, i: (bi, 0, 3)),
            pl.BlockSpec((1, 1, d), lambda bi, i: (bi, 0, 4)),
            pl.BlockSpec((e, d), lambda bi, i: (0, 0)),
        ],
        out_specs=[
            pl.BlockSpec((tm, dh), lambda bi, i: (bi * nt + i, 0)),
            pl.BlockSpec((e, tm), lambda bi, i: (0, bi * nt + i)),
        ],
        out_shape=[jax.ShapeDtypeStruct((b * l, dh), I32), jax.ShapeDtypeStruct((e, b * l), F32)],
        scratch_shapes=[pltpu.VMEM((tm, d), BF16), pltpu.VMEM((tm, d), BF16), pltpu.VMEM((tm, 1), F32)],
        compiler_params=_cparams(("parallel", "parallel")),
        name="moe_router",
    )(x, norm_g.reshape(1, d), mod, mod, w_router.T)


def _excl_cumsum_lanes(x):
    n = x.shape[1]
    lane = lax.broadcasted_iota(I32, x.shape, 1)
    inc = x
    k = 1
    while k < n:
        inc = inc + jnp.where(lane >= k, pltpu.roll(inc, k, axis=1), 0)
        k *= 2
    return inc - x


def _select_kernel(aff_ref, idx_ref, gate_ref, off_ref, pos_scr, vals_scr, racc, off_sm, sem, *, cap, ntile, tmtok):
    e, n = aff_ref.shape
    key = lax.bitcast_convert_type(aff_ref[...], I32)

    def bit_step(it, prefix):
        cand = jnp.bitwise_or(prefix, jnp.left_shift(jnp.int32(1), 30 - it))
        cnt = jnp.sum((key >= cand).astype(I32), axis=1, keepdims=True)
        return jnp.where(cnt >= cap, cand, prefix)

    thr = lax.fori_loop(0, 31, bit_step, jnp.zeros((e, 1), I32))
    gt = key > thr
    eq = key == thr
    need = cap - jnp.sum(gt.astype(I32), axis=1, keepdims=True)
    sel = jnp.logical_or(gt, jnp.logical_and(eq, _excl_cumsum_lanes(eq.astype(I32)) < need))
    sel_i = sel.astype(I32)
    pos_scr[...] = jnp.where(sel, _excl_cumsum_lanes(sel_i), -1)

    tok = lax.broadcasted_iota(I32, (e, n), 1)
    lane = lax.broadcasted_iota(I32, off_ref.shape, 1)
    off = jnp.zeros(off_ref.shape, I32)
    for b in range(1, ntile + 1):
        cnt = jnp.sum(jnp.where(tok < b * tmtok, sel_i, 0), axis=1, keepdims=True)
        off = jnp.where(lane == b, cnt, off)
    off_ref[...] = off
    off_copy = pltpu.make_async_copy(off_ref, off_sm, sem)
    off_copy.start()

    tok1 = lax.broadcasted_iota(I32, vals_scr.shape, 1)
    row = lax.broadcasted_iota(I32, vals_scr.shape, 0)
    tok_hi = jnp.right_shift(tok1, 7).astype(F32)
    tok_lo = jnp.bitwise_and(tok1, 127).astype(F32)
    off_copy.wait()

    def expert(ei, _):
        g = aff_ref[pl.ds(ei, 1), :]
        g0 = g.astype(BF16).astype(F32)
        r1 = g - g0
        g1 = r1.astype(BF16).astype(F32)
        g2 = r1 - g1
        vals_scr[...] = jnp.where(row == 0, tok_hi, jnp.where(row == 1, tok_lo, jnp.where(
            row == 2, g0, jnp.where(row == 3, g1, jnp.where(row == 4, g2, 0.0))))).astype(BF16)

        def chunk(ci, _):
            s0 = ci * SLOT_BLOCK
            slot = s0 + lax.broadcasted_iota(I32, (SLOT_BLOCK, tmtok), 0)
            racc[...] = jnp.zeros_like(racc)
            for tb in range(ntile):
                @pl.when(jnp.logical_and(off_sm[ei, tb] < s0 + SLOT_BLOCK, off_sm[ei, tb + 1] > s0))
                def _():
                    cols = pl.ds(tb * tmtok, tmtok)
                    onehot = jnp.where(pos_scr[pl.ds(ei, 1), cols] == slot, 1.0, 0.0).astype(BF16)
                    racc[...] += lax.dot_general(vals_scr[:, cols], onehot, (((1,), (1,)), ((), ())),
                                                 preferred_element_type=F32)
            r = racc[...]
            idx_ref[ei, pl.ds(ci, 1), :] = (r[0:1] * 128.0 + r[1:2]).astype(I32)
            gate_ref[ei, pl.ds(ci, 1), :] = r[2:3] + r[3:4] + r[4:5]
            return 0

        lax.fori_loop(0, cap // SLOT_BLOCK, chunk, 0)
        return 0

    lax.fori_loop(0, e, expert, 0)


def _select(aff_t, cap, ntile, tmtok):
    e, n = aff_t.shape
    nc = cap // SLOT_BLOCK
    kern = functools.partial(_select_kernel, cap=cap, ntile=ntile, tmtok=tmtok)
    idx, gate, off = pl.pallas_call(
        kern,
        out_shape=[jax.ShapeDtypeStruct((e, nc, SLOT_BLOCK), I32),
                   jax.ShapeDtypeStruct((e, nc, SLOT_BLOCK), F32),
                   jax.ShapeDtypeStruct((e, LANES), I32)],
        scratch_shapes=[pltpu.VMEM((e, n), I32), pltpu.VMEM((16, n), BF16), pltpu.VMEM((16, SLOT_BLOCK), F32),
                        pltpu.SMEM((e, LANES), I32), pltpu.SemaphoreType.DMA(())],
        compiler_params=pltpu.CompilerParams(vmem_limit_bytes=VMEM_LIMIT),
        name="moe_select",
    )(aff_t)
    return idx.reshape(e, cap), gate.reshape(e, cap), off


def _ffn_kernel(idx_sm, *refs, ngrp, cap, nf, tf, dh, rows_up, rows_down):
    hp_hbm = refs[:ngrp]
    wg_ref, wu_ref, wd_ref, o_ref, xp_scr, xs_scr, hid_scr, sem = refs[ngrp:]
    e = pl.program_id(0)
    s = pl.program_id(1)
    ne = pl.num_programs(0)
    m = ngrp * cap

    def issue_gather(expert):
        for grp in range(ngrp):
            base = (grp * ne + expert) * cap

            def body(c, _):
                pltpu.make_async_copy(hp_hbm[grp].at[pl.ds(idx_sm[base + c], 1), :],
                                      xp_scr.at[pl.ds(grp * cap + c, 1), :], sem).start()
                return 0

            lax.fori_loop(0, cap, body, 0, unroll=8)

    def wait_gather():
        pltpu.make_async_copy(hp_hbm[0].at[pl.ds(0, m), :], xp_scr, sem).wait()

    nstep = pl.num_programs(1)

    @pl.when(jnp.logical_and(e == 0, s == 0))
    def _():
        issue_gather(0)

    @pl.when(s == 0)
    def _():
        wait_gather()
        p = xp_scr[...]
        xs_scr[0] = lax.bitcast_convert_type(jnp.left_shift(p, 16), F32).astype(BF16)
        xs_scr[1] = lax.bitcast_convert_type(jnp.bitwise_and(p, jnp.int32(-65536)), F32).astype(BF16)

    nxt = jnp.minimum(e + 1, ne - 1)

    def issue_slice(first, count):
        for grp in range(ngrp):
            base = (grp * ne + nxt) * cap + first
            for c in range(count):
                pltpu.make_async_copy(hp_hbm[grp].at[pl.ds(idx_sm[base + c], 1), :],
                                      xp_scr.at[pl.ds(grp * cap + first + c, 1), :], sem).start(priority=c % 2)

    @pl.when(s < nf)
    def _():
        issue_slice(s * rows_up, rows_up)
        wg = wg_ref[0, 0].astype(BF16)
        wu = wu_ref[0, 0].astype(BF16)
        xl = xs_scr[0]
        xh = xs_scr[1]
        g = (jnp.dot(xl, wg[:dh], preferred_element_type=F32) + jnp.dot(xh, wg[dh:], preferred_element_type=F32))
        u = (jnp.dot(xl, wu[:dh], preferred_element_type=F32) + jnp.dot(xh, wu[dh:], preferred_element_type=F32))
        hid_scr[s] = (_silu(g) * u).astype(BF16)

    @pl.when(s >= nf)
    def _():
        issue_slice(nf * rows_up + (s - nf) * rows_down, rows_down)
        wd = wd_ref[0, 0].astype(BF16)
        acc = jnp.dot(hid_scr[0], wd[:tf], preferred_element_type=F32)
        for k in range(1, nf):
            acc = acc + jnp.dot(hid_scr[k], wd[k * tf:(k + 1) * tf], preferred_element_type=F32)
        for grp in range(ngrp):
            o_ref[grp, 0] = acc[grp * cap:(grp + 1) * cap]

    @pl.when(jnp.logical_and(e == ne - 1, s == nstep - 1))
    def _():
        wait_gather()


def _ffn(idx_flat, hps, w_gate, w_up, w_down, layer, cap, tf, tn):
    ngrp = len(hps)
    n, dh = hps[0].shape
    _, ne, d, f = w_gate.shape
    nf = f // tf
    nn = d // tn
    m = ngrp * cap
    assert all(hp.shape[0] >= m for hp in hps)
    assert (cap * tf) % (f + d) == 0 and (cap * tn) % (f + d) == 0
    kern = functools.partial(_ffn_kernel, ngrp=ngrp, cap=cap, nf=nf, tf=tf, dh=dh,
                             rows_up=cap * tf // (f + d), rows_down=cap * tn // (f + d))
    grid_spec = pltpu.PrefetchScalarGridSpec(
        num_scalar_prefetch=1,
        grid=(ne, nf + nn),
        in_specs=[pl.BlockSpec(memory_space=pl.ANY)] * ngrp + [
            pl.BlockSpec((1, 1, d, tf), lambda e, s, idx: (layer, e, 0, jnp.minimum(s, nf - 1))),
            pl.BlockSpec((1, 1, d, tf), lambda e, s, idx: (layer, e, 0, jnp.minimum(s, nf - 1))),
            pl.BlockSpec((1, 1, f, tn), lambda e, s, idx: (layer, e, 0, jnp.maximum(s - nf, 0))),
        ],
        out_specs=pl.BlockSpec((ngrp, 1, cap, tn), lambda e, s, idx: (0, e, 0, jnp.maximum(s - nf, 0))),
        scratch_shapes=[
            pltpu.VMEM((m, dh), I32),
            pltpu.VMEM((2, m, dh), BF16),
            pltpu.VMEM((nf, m, tf), BF16),
            pltpu.SemaphoreType.DMA(()),
        ],
    )
    return pl.pallas_call(
        kern,
        grid_spec=grid_spec,
        out_shape=jax.ShapeDtypeStruct((ngrp, ne, cap, d), F32),
        compiler_params=_cparams(("arbitrary", "arbitrary")),
        name="moe_ffn",
    )(idx_flat, *hps, w_gate, w_up, w_down)


def _worklist(off, ne, ntile, nblk):
    bs = SLOT_BLOCK
    grp_items = COMBINE_ITEMS
    w_max = ne * (nblk + ntile - 1) + ntile * grp_items
    w_max = -(-w_max // grp_items) * grp_items
    lo = off[:, :ntile].T.reshape(-1)
    hi = off[:, 1:ntile + 1].T.reshape(-1)
    b0 = jnp.minimum(lo // bs, nblk - 1)
    nb = jnp.where(hi > lo, (hi + bs - 1) // bs - b0, 0)
    e_of = jnp.arange(ntile * ne, dtype=I32) % ne
    nb = jnp.where(e_of == 0, jnp.maximum(nb, 1), nb)
    per_tile = jnp.sum(nb.reshape(ntile, ne), axis=1)
    pad = jnp.repeat((-per_tile) % grp_items, ne)
    nbp = nb + jnp.where(e_of == ne - 1, pad, 0)
    end = jnp.cumsum(nbp)
    total = end[-1]
    w = jnp.arange(w_max, dtype=I32)
    wc = jnp.minimum(w, total - 1)
    pair = jnp.sum((end[None, :] <= wc[:, None]).astype(I32), axis=1)
    table = jnp.stack([end - nbp, b0, lo, hi, nb], axis=1)
    onehot = pair[:, None] == jnp.arange(ntile * ne, dtype=I32)[None, :]
    got = jnp.sum(jnp.where(onehot[:, :, None], table[None], 0), axis=1)
    tile = pair // ne
    exp = pair % ne
    k = wc - got[:, 0]
    blk = jnp.minimum(got[:, 1] + k, nblk - 1)
    valid = jnp.logical_and(w < total, k < got[:, 4])
    rlo = jnp.where(valid, jnp.clip(got[:, 2] - blk * bs, 0, bs), 0)
    rhi = jnp.where(valid, jnp.clip(got[:, 3] - blk * bs, 0, bs), 0)
    first = jnp.concatenate([jnp.ones((1,), bool), tile[1:] != tile[:-1]])
    last = jnp.concatenate([tile[1:] != tile[:-1], jnp.ones((1,), bool)])
    to_i = lambda a: a.astype(I32)
    return to_i(tile), to_i(exp), to_i(blk), to_i(rlo), to_i(rhi), to_i(first), to_i(last)


def _combine_kernel(tile_sm, exp_sm, blk_sm, rlo_sm, rhi_sm, first_sm, last_sm, tq_sm, tu_sm, gate_sm, *refs,
                    cap, final):
    src_refs = refs[:COMBINE_ITEMS]
    x_ref, g2_ref, fg_ref, o_ref, acc_scr = refs[COMBINE_ITEMS:]
    step = pl.program_id(0)

    @pl.when(first_sm[step * COMBINE_ITEMS] == 1)
    def _():
        acc_scr[...] = jnp.zeros_like(acc_scr)

    def item(w, src_ref):
        slot0 = exp_sm[w] * cap + blk_sm[w] * SLOT_BLOCK
        rlo = rlo_sm[w]
        rhi = rhi_sm[w]
        body0 = jnp.minimum(((rlo + COMBINE_ROWS - 1) // COMBINE_ROWS) * COMBINE_ROWS, rhi)
        body1 = jnp.maximum((rhi // COMBINE_ROWS) * COMBINE_ROWS, body0)

        def row(r, _):
            q, u = tq_sm[slot0 + r], tu_sm[slot0 + r]
            acc_scr[q, pl.ds(u, 1), :] = (acc_scr[q, pl.ds(u, 1), :]
                                          + src_ref[0, 0, r // COMBINE_ROWS, pl.ds(r % COMBINE_ROWS, 1), :]
                                          * gate_sm[slot0 + r])
            return 0

        def group(gi, _):
            sq = body0 // COMBINE_ROWS + gi
            s0 = slot0 + sq * COMBINE_ROWS
            qs = [tq_sm[s0 + u] for u in range(COMBINE_ROWS)]
            us = [tu_sm[s0 + u] for u in range(COMBINE_ROWS)]
            new = [acc_scr[qs[u], pl.ds(us[u], 1), :] + src_ref[0, 0, sq, u:u + 1, :] * gate_sm[s0 + u]
                   for u in range(COMBINE_ROWS)]
            for u in range(COMBINE_ROWS):
                acc_scr[qs[u], pl.ds(us[u], 1), :] = new[u]
            return 0

        lax.fori_loop(rlo, body0, row, 0)
        lax.fori_loop(0, (body1 - body0) // COMBINE_ROWS, group, 0)
        lax.fori_loop(body1, rhi, row, 0)

    for j in range(COMBINE_ITEMS):
        item(step * COMBINE_ITEMS + j, src_refs[j])

    @pl.when(last_sm[step * COMBINE_ITEMS + COMBINE_ITEMS - 1] == 1)
    def _():
        xn = x_ref[0] + g2_ref[0] * acc_scr[...]
        if final:
            ms = jnp.mean(xn * xn, axis=-1, keepdims=True)
            xn = xn * lax.rsqrt(ms + RMS_EPS) * fg_ref[...]
        o_ref[0] = xn


def _combine(work, idx_flat_g, gate_flat_g, outs, grp, x, mod, final_g, tmtok, final):
    b, l, d = x.shape
    ngrp, ne, cap = outs.shape[:3]
    tpb = l // tmtok
    w_max = work[0].shape[0]
    rs = COMBINE_ROWS
    kern = functools.partial(_combine_kernel, cap=cap, final=final)
    tloc = idx_flat_g % tmtok
    outs_r = outs.reshape(ngrp, ne, cap // rs, rs, d)
    x_r = x.reshape(b, l // rs, rs, d)
    tile_rows = tmtok // rs
    gi = COMBINE_ITEMS

    def src_spec(j):
        return pl.BlockSpec((1, 1, SLOT_BLOCK // rs, rs, d),
                            lambda w, ti, ex, bl, *_: (grp, ex[w * gi + j], bl[w * gi + j], 0, 0))

    def tile_of(w, ti):
        t = ti[w * gi]
        return t // tpb, t % tpb

    grid_spec = pltpu.PrefetchScalarGridSpec(
        num_scalar_prefetch=10,
        grid=(w_max // gi,),
        in_specs=[src_spec(j) for j in range(gi)] + [
            pl.BlockSpec((1, tile_rows, rs, d), lambda w, ti, *_: tile_of(w, ti) + (0, 0),
                         pipeline_mode=pl.Buffered(1)),
            pl.BlockSpec((1, 1, d), lambda w, ti, *_: (tile_of(w, ti)[0], 0, 5)),
            pl.BlockSpec((1, d), lambda w, *_: (0, 0)),
        ],
        out_specs=pl.BlockSpec((1, tile_rows, rs, d), lambda w, ti, *_: tile_of(w, ti) + (0, 0)),
        scratch_shapes=[pltpu.VMEM((tile_rows, rs, d), F32)],
    )
    out = pl.pallas_call(
        kern,
        grid_spec=grid_spec,
        out_shape=jax.ShapeDtypeStruct((b, l // rs, rs, d), F32),
        compiler_params=_cparams(("arbitrary",)),
        name="moe_combine",
    )(*work, tloc // rs, tloc % rs, gate_flat_g, *([outs_r] * gi), x_r, mod, final_g.reshape(1, d))
    return out.reshape(b, l, d)


def _pick(n, pref):
    t = min(pref, n)
    while n % t:
        t //= 2
    return t


def _moe_layer(xs, mods, norm_g, w_router, w_gate, w_up, w_down, layer, final_g, final):
    ne = w_router.shape[1]
    hps, idxs, gates, works = [], [], [], []
    caps = set()
    for x, mod in zip(xs, mods):
        b, l, d = x.shape
        n = b * l
        cap = EC_CAPACITY_FACTOR * n // ne
        caps.add(cap)
        tmtok = _pick(l, 1024)
        ntile = n // tmtok
        hp, aff_t = _router(x, norm_g, mod, w_router, _pick(l, 512))
        idx, gate, off = _select(aff_t, cap, ntile, tmtok)
        hps.append(hp)
        idxs.append(idx.reshape(-1))
        gates.append(gate.reshape(-1))
        works.append(_worklist(off, ne, ntile, cap // SLOT_BLOCK))
    assert len(caps) == 1, "request groups must have equal expert capacity"
    cap = caps.pop()
    f = w_gate.shape[3]
    outs = _ffn(jnp.concatenate(idxs), hps, w_gate, w_up, w_down, layer, cap,
                _pick(f, 256), _pick(xs[0].shape[2], 256))
    new = []
    for grp, (x, mod) in enumerate(zip(xs, mods)):
        tmtok = _pick(x.shape[1], 1024)
        new.append(_combine(works[grp], idxs[grp], gates[grp], outs, grp, x, mod, final_g, tmtok, final))
    return new


def kernel(x_prompt, x_sample, c_prompt, c_sample, ada_w, ada_b, norm_mix_g, norm_ffn_g, final_norm_g, conv_w_in, conv_b_in, conv_w_dw, conv_b_dw, conv_ln_g, conv_ln_b, conv_w_out, conv_b_out, ssm_lambda_re, ssm_lambda_im, ssm_log_step, ssm_b_re, ssm_b_im, ssm_c_re, ssm_c_im, ssm_d, ssm_w_glu, ssm_b_glu, moe_w_router, moe_w_gate, moe_w_up, moe_w_down):
    depth = ada_w.shape[0]
    d = x_prompt.shape[-1]
    bp, bs = x_prompt.shape[0], x_sample.shape[0]
    nrow = 16 * ((bp + bs + 15) // 16)
    c_all = jnp.concatenate([c_prompt, c_sample, jnp.zeros((nrow - bp - bs, d), F32)], axis=0)
    mod_all = _ada(c_all, ada_w, ada_b)

    xs = [x_prompt, x_sample]
    for i in range(depth):
        mods = [mod_all[i, :bp].reshape(bp, 1, 6 * d), mod_all[i, bp:bp + bs].reshape(bs, 1, 6 * d)]
        j = i // 2
        if i % 2 == 0:
            w_in = conv_w_in[j].astype(BF16)
            w_out = conv_w_out[j].astype(BF16)
            nxt = []
            for x, mod in zip(xs, mods):
                l = x.shape[1]
                u = _conv_in(x, norm_mix_g[i], mod, w_in, conv_b_in[j], _pick(l, 1024), _pick(d, 1024))
                nxt.append(_conv_out(u, x, mod, conv_w_dw[j], conv_b_dw[j], conv_ln_g[j], conv_ln_b[j],
                                     w_out, conv_b_out[j], _pick(l, 512), _pick(d, 1024)))
            xs = nxt
        else:
            lay = _s5_layout(ssm_lambda_re[j], ssm_lambda_im[j], ssm_log_step[j], ssm_b_re[j], ssm_b_im[j],
                             ssm_c_re[j], ssm_c_im[j])
            w_glu = ssm_w_glu[j].astype(BF16)
            nxt = []
            for x, mod in zip(xs, mods):
                l = x.shape[1]
                y2 = _s5_scan(x, norm_mix_g[i], mod, lay, _pick(l, 256))
                nxt.append(_s5_glu(y2, x, norm_mix_g[i], mod, ssm_d[j], w_glu, ssm_b_glu[j],
                                   _pick(l, 512), _pick(d, 1024)))
            xs = nxt
        xs = _moe_layer(xs, mods, norm_ffn_g[i], moe_w_router[i], moe_w_gate, moe_w_up, moe_w_down, i,
                        final_norm_g, final=(i == depth - 1))
    return (xs[0], xs[1])
```
